```python
import jax, jax.numpy as jnp
from jax import lax
import numpy as np

D_MODEL = 1024
BATCH = 8
SEQ = 4096
DEPTH = 2

RET_HEADS = 8
RET_DK = 64
RET_DV = D_MODEL // RET_HEADS
RET_CHUNK = 128
RET_ROPE_BASE = 10000.0
ATT_HD = 64
ATT_HEADS = D_MODEL // ATT_HD
ATT_KV_HEADS = 4
WINDOW = 128
ATT_BLOCK = 128
ROPE_DIMS = ATT_HD // 4
ROPE_THETA = 500000.0
PEER_HEADS = 8
PEER_NKEYS = 128
PEER_EXPERTS = PEER_NKEYS * PEER_NKEYS
PEER_QDIM = 256
PEER_TOPK = 16
PEER_CHUNK = 128

EPS = 1e-6
NEG_INF = -1e30

kernel_name = "hybrid_retention_swa_peer_adaln"


def rmsnorm(x, g):
    xf = x.astype(jnp.float32)
    y = xf * lax.rsqrt(jnp.mean(xf * xf, axis=-1, keepdims=True) + EPS)
    return (y * g.astype(jnp.float32)).astype(x.dtype)


def rotary(t, positions, n_rot, base):
    half = n_rot // 2
    inv = jnp.power(base, -jnp.arange(half, dtype=jnp.float32) * (2.0 / n_rot))
    ang = positions.astype(jnp.float32)[..., None] * inv
    cos = jnp.cos(ang)[:, :, None, :].astype(t.dtype)
    sin = jnp.sin(ang)[:, :, None, :].astype(t.dtype)
    t1 = t[..., :half]
    t2 = t[..., half:n_rot]
    rot = jnp.concatenate([t1 * cos - t2 * sin, t2 * cos + t1 * sin], axis=-1)
    if n_rot == t.shape[-1]:
        return rot
    return jnp.concatenate([rot, t[..., n_rot:]], axis=-1)


def retention(q, k, v):
    B, S, H, dk = q.shape
    dv = v.shape[-1]
    C = RET_CHUNK
    N = S // C
    dt = q.dtype
    k = k * (dk ** -0.5)
    qc = q.reshape(B, N, C, H, dk)
    kc = k.reshape(B, N, C, H, dk)
    vc = v.reshape(B, N, C, H, dv)
    log_g = jnp.log(1.0 - jnp.power(2.0, -5.0 - jnp.arange(H, dtype=jnp.float32)))
    idx = jnp.arange(C, dtype=jnp.float32)
    diff = idx[:, None] - idx[None, :]
    decay = jnp.where(diff >= 0, jnp.exp(log_g[:, None, None] * jnp.maximum(diff, 0.0)), 0.0)
    scores = jnp.einsum('bnchd,bnmhd->bnhcm', qc, kc) * decay.astype(dt)
    inner = jnp.einsum('bnhcm,bnmhe->bnche', scores, vc)
    k_w = jnp.exp(log_g[:, None] * (C - 1.0 - idx)[None, :]).astype(dt)
    kv = jnp.einsum('bnmhd,bnmhe,hm->nbhde', kc, vc, k_w)
    chunk_decay = jnp.exp(log_g * C).astype(kv.dtype)[:, None, None]

    def step(state, kv_i):
        return state * chunk_decay + kv_i, state

    _, prev = lax.scan(step, jnp.zeros((B, H, dk, dv), kv.dtype), kv)
    q_w = jnp.exp(log_g[:, None] * (idx + 1.0)[None, :]).astype(dt)
    cross = jnp.einsum('bnchd,nbhde,hc->bnche', qc, prev, q_w)
    return (inner + cross).reshape(B, S, H, dv)


def head_groupnorm(o, g):
    B, S, H, dv = o.shape
    of = o.astype(jnp.float32)
    mu = jnp.mean(of, axis=-1, keepdims=True)
    var = jnp.mean(jnp.square(of - mu), axis=-1, keepdims=True)
    y = ((of - mu) * lax.rsqrt(var + EPS)).reshape(B, S, H * dv)
    return (y * g.astype(jnp.float32)).astype(o.dtype)


def sliding_window_attention(q, k, v, sinks):
    B, S, HQ, Dh = q.shape
    HKV = k.shape[2]
    G = HQ // HKV
    L = ATT_BLOCK
    N = S // L
    qb = (q * (Dh ** -0.5)).reshape(B, N, L, HKV, G, Dh)
    kb = k.reshape(B, N, L, HKV, Dh)
    vb = v.reshape(B, N, L, HKV, Dh)

    def with_prev(t):
        prev = jnp.concatenate([jnp.zeros_like(t[:, :1]), t[:, :-1]], axis=1)
        return jnp.concatenate([prev, t], axis=2)

    kk = with_prev(kb)
    vv = with_prev(vb)
    s = jnp.einsum('bnqhgd,bnkhd->bnhgqk', qb, kk).astype(jnp.float32)
    qi = jnp.arange(L)[:, None]
    ki = jnp.arange(2 * L)[None, :]
    rel = qi + L - ki
    kpos = jnp.arange(N)[:, None, None] * L - L + ki[None]
    valid = (rel >= 0)[None] & (rel < WINDOW)[None] & (kpos >= 0)
    s = jnp.where(valid[None, :, None, None], s, NEG_INF)
    sink = sinks.astype(jnp.float32).reshape(1, 1, HKV, G, 1, 1)
    m = jnp.maximum(jnp.max(s, axis=-1, keepdims=True), sink)
    p = jnp.exp(s - m)
    p = p / (jnp.sum(p, axis=-1, keepdims=True) + jnp.exp(sink - m))
    o = jnp.einsum('bnhgqk,bnkhd->bnqhgd', p.astype(v.dtype), vv)
    return o.reshape(B, S, HQ * Dh)


def token_mixer(h, positions, w_in, ret_gn_g, sinks, w_out):
    B, S, _ = h.shape
    widths = (RET_HEADS * RET_DK, RET_HEADS * RET_DK, RET_HEADS * RET_DV, RET_HEADS * RET_DV,
              ATT_HEADS * ATT_HD, ATT_KV_HEADS * ATT_HD, ATT_KV_HEADS * ATT_HD, D_MODEL, D_MODEL)
    cuts = [int(c) for c in np.cumsum(widths)[:-1]]
    proj = h @ w_in
    rq, rk, rv, rg, aq, ak, av, gate_a, gate_b = jnp.split(proj, cuts, axis=-1)
    rq = rotary(rq.reshape(B, S, RET_HEADS, RET_DK), positions, RET_DK, RET_ROPE_BASE)
    rk = rotary(rk.reshape(B, S, RET_HEADS, RET_DK), positions, RET_DK, RET_ROPE_BASE)
    ro = retention(rq, rk, rv.reshape(B, S, RET_HEADS, RET_DV))
    ya = head_groupnorm(ro, ret_gn_g) * jax.nn.silu(rg)
    aq = rotary(aq.reshape(B, S, ATT_HEADS, ATT_HD), positions, ROPE_DIMS, ROPE_THETA)
    ak = rotary(ak.reshape(B, S, ATT_KV_HEADS, ATT_HD), positions, ROPE_DIMS, ROPE_THETA)
    yb = sliding_window_attention(aq, ak, av.reshape(B, S, ATT_KV_HEADS, ATT_HD), sinks)
    merged = jax.nn.sigmoid(gate_a) * ya + jax.nn.sigmoid(gate_b) * yb
    return merged @ w_out


def peer(h, wq, k1, k2, u, v):
    B, S, D = h.shape
    H, K, half = PEER_HEADS, PEER_TOPK, PEER_QDIM // 2
    q = (h @ wq).reshape(B, S, H, PEER_QDIM)
    s1 = jnp.einsum('bshd,hnd->bshn', q[..., :half], k1).astype(jnp.float32)
    s2 = jnp.einsum('bshd,hnd->bshn', q[..., half:], k2).astype(jnp.float32)
    v1, i1 = lax.top_k(s1, K)
    v2, i2 = lax.top_k(s2, K)
    cand = (v1[..., :, None] + v2[..., None, :]).reshape(B, S, H, K * K)
    cand_idx = (i1[..., :, None] * PEER_NKEYS + i2[..., None, :]).reshape(B, S, H, K * K)
    top_s, pos = lax.top_k(cand, K)
    experts = jnp.take_along_axis(cand_idx, pos, axis=-1)
    gates = jax.nn.softmax(top_s, axis=-1).astype(h.dtype)
    n_chunk = (B * S) // PEER_CHUNK
    hx = h.reshape(n_chunk, PEER_CHUNK, D)
    ex = experts.reshape(n_chunk, PEER_CHUNK, H * K)
    gx = gates.reshape(n_chunk, PEER_CHUNK, H * K)

    def one_chunk(args):
        xc, ec, gc = args
        act = jax.nn.gelu(jnp.einsum('td,tkd->tk', xc, u[ec]), approximate=False)
        return jnp.einsum('tk,tkd->td', act * gc, v[ec])

    y = lax.map(one_chunk, (hx, ex, gx))
    return y.reshape(B, S, D)


def setup_inputs(seed: int = 0) -> dict:
    key = jax.random.key(seed)
    ks = jax.random.split(key, 18)
    D = D_MODEL
    in_w = 2 * RET_HEADS * RET_DK + 2 * RET_HEADS * RET_DV + ATT_HEADS * ATT_HD + 2 * ATT_KV_HEADS * ATT_HD + 2 * D
    nrm = jax.random.normal
    x = nrm(ks[0], (BATCH, SEQ, D), jnp.float32)
    c = nrm(ks[1], (BATCH, D), jnp.float32)
    positions = (jnp.arange(SEQ, dtype=jnp.int32)[None, :]
                 + jax.random.randint(ks[2], (BATCH, 1), 0, 1024, dtype=jnp.int32))
    ada_w = nrm(ks[3], (DEPTH, D, 6 * D), jnp.float32) * (0.2 * D ** -0.5)
    ada_b = nrm(ks[4], (DEPTH, 6 * D), jnp.float32) * 0.1
    norm1_g = 1.0 + 0.05 * nrm(ks[5], (DEPTH, D), jnp.float32)
    norm2_g = 1.0 + 0.05 * nrm(ks[6], (DEPTH, D), jnp.float32)
    w_in = nrm(ks[7], (DEPTH, D, in_w), jnp.float32) * D ** -0.5
    ret_gn_g = 1.0 + 0.05 * nrm(ks[8], (DEPTH, RET_HEADS * RET_DV), jnp.float32)
    sinks = 0.5 * nrm(ks[9], (DEPTH, ATT_HEADS), jnp.float32)
    w_out = nrm(ks[10], (DEPTH, D, D), jnp.float32) * D ** -0.5
    peer_wq = nrm(ks[11], (DEPTH, D, PEER_HEADS * PEER_QDIM), jnp.float32) * D ** -0.5
    peer_k1 = nrm(ks[12], (DEPTH, PEER_HEADS, PEER_NKEYS, PEER_QDIM // 2), jnp.float32) * (PEER_QDIM // 2) ** -0.5
    peer_k2 = nrm(ks[13], (DEPTH, PEER_HEADS, PEER_NKEYS, PEER_QDIM // 2), jnp.float32) * (PEER_QDIM // 2) ** -0.5
    peer_u = nrm(ks[14], (DEPTH, PEER_EXPERTS, D), jnp.float32) * D ** -0.5
    peer_v = nrm(ks[15], (DEPTH, PEER_EXPERTS, D), jnp.float32) * PEER_HEADS ** -0.5
    final_g = 1.0 + 0.05 * nrm(ks[16], (D,), jnp.float32)
    return {"x": x, "c": c, "positions": positions, "ada_w": ada_w, "ada_b": ada_b,
            "norm1_g": norm1_g, "norm2_g": norm2_g, "w_in": w_in, "ret_gn_g": ret_gn_g,
            "sinks": sinks, "w_out": w_out, "peer_wq": peer_wq, "peer_k1": peer_k1,
            "peer_k2": peer_k2, "peer_u": peer_u, "peer_v": peer_v, "final_g": final_g}


def reference(x, c, positions, ada_w, ada_b, norm1_g, norm2_g, w_in, ret_gn_g, sinks, w_out,
              peer_wq, peer_k1, peer_k2, peer_u, peer_v, final_g):
    cond = jax.nn.silu(c)
    for l in range(DEPTH):
        mod = cond @ ada_w[l] + ada_b[l]
        sh1, sc1, g1, sh2, sc2, g2 = jnp.split(mod, 6, axis=-1)
        h = rmsnorm(x, norm1_g[l]) * (1.0 + sc1[:, None]) + sh1[:, None]
        x = x + g1[:, None] * token_mixer(h, positions, w_in[l], ret_gn_g[l], sinks[l], w_out[l])
        h = rmsnorm(x, norm2_g[l]) * (1.0 + sc2[:, None]) + sh2[:, None]
        x = x + g2[:, None] * peer(h, peer_wq[l], peer_k1[l], peer_k2[l], peer_u[l], peer_v[l])
    return rmsnorm(x, final_g)
```

```python
import functools

import numpy as np
import jax
import jax.numpy as jnp
from jax import lax
from jax.experimental import pallas as pl
from jax.experimental.pallas import tpu as pltpu

F32 = jnp.float32
BF16 = jnp.bfloat16

D_MODEL = 1024
RET_HEADS = 8
RET_DK = 64
RET_DV = 128
RET_ROPE_BASE = 10000.0
ATT_HD = 64
ATT_HEADS = 16
ATT_KV_HEADS = 4
ROPE_DIMS = 16
ROPE_THETA = 500000.0
BLOCK = 128
PEER_HEADS = 8
PEER_NKEYS = 128
PEER_TOPK = 16
PEER_QDIM = 256
EPS = 1e-6
NEG_INF = -1e30
LANES = 128
SUBLANES = 8
VMEM_LIMIT = 56 * 1024 * 1024

IN_WIDTHS = (512, 512, 1024, 1024, 1024, 256, 256, 1024, 1024)

_NT = (((1,), (1,)), ((), ()))
_TN = (((0,), (0,)), ((), ()))


def _pair_lane(slot, pos):
    return (pos // 32) * 64 + slot * 32 + pos % 32


def _pair_perm(n_heads, dim_order):
    src = np.zeros(n_heads * 64, np.int32)
    for p in range(n_heads // 2):
        for slot in range(2):
            for pos in range(64):
                src[p * LANES + _pair_lane(slot, pos)] = (2 * p + slot) * 64 + dim_order[pos]
    return src


def _att_dim_order():
    return np.concatenate([np.arange(0, 8), np.arange(16, 40), np.arange(8, 16), np.arange(40, 64)])


def _in_col_perm():
    ret = _pair_perm(RET_HEADS, np.arange(64))
    att_q = _pair_perm(ATT_HEADS, _att_dim_order())
    att_k = _pair_perm(ATT_KV_HEADS, _att_dim_order())
    starts = np.concatenate([[0], np.cumsum(IN_WIDTHS)[:-1]])
    cols = np.arange(sum(IN_WIDTHS), dtype=np.int32)
    cols[starts[0]:starts[0] + 512] = starts[0] + ret
    cols[starts[1]:starts[1] + 512] = starts[1] + ret
    cols[starts[4]:starts[4] + 1024] = starts[4] + att_q
    cols[starts[5]:starts[5] + 256] = starts[5] + att_k
    return cols


def _slot_masks():
    lane = np.arange(LANES)
    slot = (lane // 32) % 2
    return np.stack([(slot == 0), (slot == 1)]).astype(np.float32)


def _retention_consts():
    h = np.arange(RET_HEADS, dtype=np.float64)
    log_g = np.log(1.0 - np.power(2.0, -5.0 - h))
    idx = np.arange(BLOCK, dtype=np.float64)
    diff = idx[:, None] - idx[None, :]
    decay = np.where(diff >= 0, np.exp(log_g[:, None, None] * np.maximum(diff, 0.0)), 0.0)
    k_w = np.exp(log_g[:, None] * (BLOCK - 1.0 - idx)[None, :])
    q_w = np.exp(log_g[:, None] * (idx + 1.0)[None, :])
    cdec = np.exp(log_g * BLOCK)
    masks = _slot_masks().astype(np.float64)
    qwm = np.stack([q_w[hh][:, None] * masks[hh % 2][None, :] for hh in range(RET_HEADS)])
    kwm = np.stack([k_w[hh][:, None] * masks[hh % 2][None, :] * (RET_DK ** -0.5)
                    for hh in range(RET_HEADS)])
    row_slot = (np.arange(LANES) // 32) % 2
    cdm = np.stack([np.broadcast_to(cdec[2 * p + row_slot][:, None], (LANES, LANES))
                    for p in range(RET_HEADS // 2)])
    f = lambda a: jnp.asarray(a.astype(np.float32))
    return f(decay), f(qwm), f(kwm), f(cdm)


def _rotary_tables(positions):
    pos = positions.reshape(-1).astype(F32)
    lane = np.arange(LANES)
    hpos = (lane // 64) * 32 + lane % 32
    sign = np.where(lane < 64, -1.0, 1.0).astype(np.float32)
    half = RET_DK // 2
    inv_r = jnp.power(RET_ROPE_BASE, -jnp.arange(half, dtype=F32) * (2.0 / RET_DK))
    ang_r = pos[:, None] * inv_r[None, :]
    fr = (hpos % 32)
    cr = jnp.cos(ang_r)[:, fr]
    sr = jnp.sin(ang_r)[:, fr] * sign[None, :]
    half_a = ROPE_DIMS // 2
    inv_a = jnp.power(ROPE_THETA, -jnp.arange(half_a, dtype=F32) * (2.0 / ROPE_DIMS))
    ang_a = pos[:, None] * inv_a[None, :]
    is_rot = (hpos % 32) < half_a
    fa = np.where(is_rot, hpos % 32, 0)
    ca = jnp.where(is_rot[None, :], jnp.cos(ang_a)[:, fa], 1.0)
    sa = jnp.where(is_rot[None, :], jnp.sin(ang_a)[:, fa] * sign[None, :], 0.0)
    return cr, sr, ca, sa


def _adaln_kernel(c_ref, w_ref, b_ref, o_ref):
    c = c_ref[...]
    cond = c * jax.nn.sigmoid(c)
    o_ref[0] = jnp.dot(cond, w_ref[0], preferred_element_type=F32,
                       precision=lax.Precision.HIGHEST) + b_ref[0]


def _adaln(c, ada_w, ada_b):
    depth, d, n = ada_w.shape
    b = c.shape[0]
    return pl.pallas_call(
        _adaln_kernel,
        grid=(depth, n // d),
        in_specs=[pl.BlockSpec((b, d), lambda l, j: (0, 0)),
                  pl.BlockSpec((1, d, d), lambda l, j: (l, 0, j)),
                  pl.BlockSpec((1, 1, d), lambda l, j: (l, 0, j))],
        out_specs=pl.BlockSpec((1, b, d), lambda l, j: (l, 0, j)),
        out_shape=jax.ShapeDtypeStruct((depth, b, n), F32),
        name="adaln",
    )(c, ada_w, ada_b.reshape(depth, 1, n))


def _modulated_norm(x, g, sc, sh):
    y = x * lax.rsqrt(jnp.mean(x * x, axis=-1, keepdims=True) + EPS)
    return (y * g) * (1.0 + sc) + sh


def _proj_kernel(x_ref, g_ref, sc_ref, sh_ref, w_ref, *out_refs, widths):
    hb = _modulated_norm(x_ref[...], g_ref[...], sc_ref[0], sh_ref[0]).astype(BF16)
    off = 0
    for o_ref, wd in zip(out_refs, widths):
        o_ref[...] = jnp.dot(hb, w_ref[:, off:off + wd],
                             preferred_element_type=F32).astype(o_ref.dtype)
        off += wd


def _norm_proj(x, g, sc, sh, w, widths, seq, tm, out_dtype):
    t, d = x.shape
    n = w.shape[1]
    bidx = lambda i: ((i * tm) // seq, 0, 0)
    return pl.pallas_call(
        functools.partial(_proj_kernel, widths=widths),
        grid=(t // tm,),
        in_specs=[pl.BlockSpec((tm, d), lambda i: (i, 0)),
                  pl.BlockSpec((1, d), lambda i: (0, 0)),
                  pl.BlockSpec((1, 1, d), bidx),
                  pl.BlockSpec((1, 1, d), bidx),
                  pl.BlockSpec((d, n), lambda i: (0, 0))],
        out_specs=[pl.BlockSpec((tm, wd), lambda i: (i, 0)) for wd in widths],
        out_shape=[jax.ShapeDtypeStruct((t, wd), out_dtype) for wd in widths],
        compiler_params=pltpu.CompilerParams(dimension_semantics=("parallel",),
                                             vmem_limit_bytes=VMEM_LIMIT),
        name="norm_proj",
    )(x, g, sc, sh, w)


def _rot(t, c, s):
    return t * c + pltpu.roll(t, 64, 1) * s


def _mixer_kernel(sinks_ref, rq_ref, rk_ref, rv_ref, rg_ref, aq_ref, ak_ref, av_ref,
                  ga_ref, gb_ref, cr_ref, sr_ref, ca_ref, sa_ref,
                  decay_ref, qwm_ref, kwm_ref, cdm_ref, msk_ref, gn_ref,
                  out_ref, state_ref, pk_ref, pv_ref, *, n_chunks):
    step = pl.program_id(1)

    @pl.when(step == 0)
    def _():
        state_ref[...] = jnp.zeros_like(state_ref)
        pk_ref[...] = jnp.zeros_like(pk_ref)
        pv_ref[...] = jnp.zeros_like(pv_ref)

    def chunk(c, carry):
        rows = pl.ds(pl.multiple_of(c * BLOCK, BLOCK), BLOCK)
        lane = lax.broadcasted_iota(jnp.int32, (BLOCK, LANES), 1)
        slot_is = [((lane // 32) % 2) == 0, ((lane // 32) % 2) == 1]
        low_half = lane < 64
        m_slot = [msk_ref[0:1, :], msk_ref[1:2, :]]
        row = lax.broadcasted_iota(jnp.int32, (BLOCK, 2 * BLOCK), 0)
        col = lax.broadcasted_iota(jnp.int32, (BLOCK, 2 * BLOCK), 1)
        first_key = jnp.where(jnp.logical_and(step == 0, c == 0), BLOCK, 0)
        lo = jnp.maximum(row + 1, first_key)
        valid = jnp.logical_and(col >= lo, col <= row + BLOCK)
        cr = cr_ref[rows, :]
        sr = sr_ref[rows, :]
        ca = ca_ref[rows, :]
        sa = sa_ref[rows, :]

        ya = []
        for p in range(RET_HEADS // 2):
            cs = slice(p * LANES, (p + 1) * LANES)
            q = _rot(rq_ref[rows, cs], cr, sr)
            k = _rot(rk_ref[rows, cs], cr, sr)
            qs = jnp.concatenate([q * m_slot[0], q * m_slot[1]], axis=0).astype(BF16)
            kb = (k * (RET_DK ** -0.5)).astype(BF16)
            sc = lax.dot_general(qs, kb, _NT, preferred_element_type=F32)
            st = state_ref[p]
            stb = st.astype(BF16)
            vbs = []
            for s in range(2):
                h = 2 * p + s
                hs = slice(h * RET_DV, (h + 1) * RET_DV)
                vb = rv_ref[rows, hs].astype(BF16)
                vbs.append(vb)
                sch = sc[s * BLOCK:(s + 1) * BLOCK] * decay_ref[h]
                qc = q * qwm_ref[h]
                lhs = jnp.concatenate([sch, qc], axis=1).astype(BF16)
                rhs = jnp.concatenate([vb, stb], axis=0)
                o = jnp.dot(lhs, rhs, preferred_element_type=F32)
                mu = jnp.mean(o, axis=-1, keepdims=True)
                dlt = o - mu
                var = jnp.mean(dlt * dlt, axis=-1, keepdims=True)
                y = (dlt * lax.rsqrt(var + EPS)) * gn_ref[:, hs]
                rg = rg_ref[rows, hs]
                ya.append(y * (rg * jax.nn.sigmoid(rg)))
            kw = jnp.concatenate([k * kwm_ref[2 * p], k * kwm_ref[2 * p + 1]], axis=0).astype(BF16)
            vst = jnp.concatenate(vbs, axis=0)
            upd = lax.dot_general(kw, vst, _TN, preferred_element_type=F32)
            state_ref[p] = st * cdm_ref[p] + upd

        yb = [None] * (ATT_HEADS // 2)
        kps, vps = [], []
        for pp in range(ATT_KV_HEADS // 2):
            cs = slice(pp * LANES, (pp + 1) * LANES)
            kps.append(_rot(ak_ref[rows, cs], ca, sa))
            vps.append(av_ref[rows, cs])
        for g in range(ATT_KV_HEADS):
            kp, vp = kps[g // 2], vps[g // 2]
            if g % 2 == 0:
                kd = jnp.where(slot_is[0], kp, pltpu.roll(kp, 32, 1))
                vd = jnp.where(low_half, vp, pltpu.roll(vp, 64, 1))
            else:
                kd = jnp.where(slot_is[1], kp, pltpu.roll(kp, 96, 1))
                vd = jnp.where(low_half, pltpu.roll(vp, 64, 1), vp)
            kdb = kd.astype(BF16)
            vdb = vd.astype(BF16)
            kcat = jnp.concatenate([pk_ref[g], kdb], axis=0)
            vcat = jnp.concatenate([pv_ref[g], vdb], axis=0)
            pk_ref[g] = kdb
            pv_ref[g] = vdb
            qparts = []
            for tt in range(2):
                t = 2 * g + tt
                qt = _rot(aq_ref[rows, t * LANES:(t + 1) * LANES], ca, sa)
                for s in range(2):
                    qparts.append(qt * (m_slot[s] * (ATT_HD ** -0.5)))
            qs = jnp.concatenate(qparts, axis=0).astype(BF16)
            s_all = lax.dot_general(qs, kcat, _NT, preferred_element_type=F32)
            pparts = []
            for r in range(4):
                sr_ = jnp.where(valid, s_all[r * BLOCK:(r + 1) * BLOCK], NEG_INF)
                sink = sinks_ref[4 * g + r]
                m = jnp.maximum(jnp.max(sr_, axis=-1, keepdims=True), sink)
                pe = jnp.exp(sr_ - m)
                den = jnp.sum(pe, axis=-1, keepdims=True) + jnp.exp(sink - m)
                pparts.append(pe / den)
            pb = jnp.concatenate(pparts, axis=0).astype(BF16)
            o = jnp.dot(pb, vcat, preferred_element_type=F32)
            for tt in range(2):
                yb[2 * g + tt] = jnp.where(low_half, o[(2 * tt) * BLOCK:(2 * tt + 1) * BLOCK],
                                           o[(2 * tt + 1) * BLOCK:(2 * tt + 2) * BLOCK])

        for ct in range(D_MODEL // LANES):
            cs = slice(ct * LANES, (ct + 1) * LANES)
            mrg = jax.nn.sigmoid(ga_ref[rows, cs]) * ya[ct] + jax.nn.sigmoid(gb_ref[rows, cs]) * yb[ct]
            out_ref[rows, cs] = mrg.astype(out_ref.dtype)
        return carry

    lax.fori_loop(0, n_chunks, chunk, 0)


def _mixer(proj, tables, consts, sinks, gn, batch, seq, ts):
    rq, rk, rv, rg, aq, ak, av, ga, gb = proj
    cr, sr, ca, sa = tables
    decay, qwm, kwm, cdm, msk = consts
    t = rq.shape[0]
    nsteps = seq // ts
    row_spec = lambda w: pl.BlockSpec((ts, w), lambda b, n: (b * nsteps + n, 0))
    full = lambda a: pl.BlockSpec(a.shape, lambda b, n: (0,) * a.ndim)
    return pl.pallas_call(
        functools.partial(_mixer_kernel, n_chunks=ts // BLOCK),
        grid=(batch, nsteps),
        in_specs=[pl.BlockSpec(memory_space=pltpu.SMEM)]
        + [row_spec(a.shape[1]) for a in (rq, rk, rv, rg, aq, ak, av, ga, gb, cr, sr, ca, sa)]
        + [full(a) for a in (decay, qwm, kwm, cdm, msk, gn)],
        out_specs=row_spec(D_MODEL),
        out_shape=jax.ShapeDtypeStruct((t, D_MODEL), BF16),
        scratch_shapes=[pltpu.VMEM((RET_HEADS // 2, LANES, RET_DV), F32),
                        pltpu.VMEM((ATT_KV_HEADS, BLOCK, LANES), BF16),
                        pltpu.VMEM((ATT_KV_HEADS, BLOCK, LANES), BF16)],
        compiler_params=pltpu.CompilerParams(dimension_semantics=("parallel", "arbitrary"),
                                             vmem_limit_bytes=VMEM_LIMIT),
        name="mixer",
    )(sinks, rq, rk, rv, rg, aq, ak, av, ga, gb, cr, sr, ca, sa, decay, qwm, kwm, cdm, msk, gn)


def _outproj_kernel(m_ref, w_ref, x_ref, g_ref, o_ref):
    o_ref[...] = x_ref[...] + g_ref[0] * jnp.dot(m_ref[...], w_ref[...],
                                                 preferred_element_type=F32)


def _outproj(merged, w, x, gate, seq, tm):
    t, d = x.shape
    return pl.pallas_call(
        _outproj_kernel,
        grid=(t // tm,),
        in_specs=[pl.BlockSpec((tm, d), lambda i: (i, 0)),
                  pl.BlockSpec((d, d), lambda i: (0, 0)),
                  pl.BlockSpec((tm, d), lambda i: (i, 0)),
                  pl.BlockSpec((1, 1, d), lambda i: ((i * tm) // seq, 0, 0))],
        out_specs=pl.BlockSpec((tm, d), lambda i: (i, 0)),
        out_shape=jax.ShapeDtypeStruct((t, d), F32),
        compiler_params=pltpu.CompilerParams(dimension_semantics=("parallel",),
                                             vmem_limit_bytes=VMEM_LIMIT),
        name="outproj",
    )(merged, w, x, gate)


def _top_values(s, k):
    vals = []
    work = s
    for _ in range(k):
        m = jnp.max(work, axis=0, keepdims=True)
        vals.append(m)
        work = jnp.where(work == m, -jnp.inf, work)
    return vals


def _peer_score_kernel(x_ref, g_ref, sc_ref, sh_ref, wq_ref, k1_ref, k2_ref,
                       ht_ref, s1_ref, s2_ref, e1_ref, e2_ref, thr_ref, q_scr):
    h = _modulated_norm(x_ref[...], g_ref[...], sc_ref[0], sh_ref[0])
    ht_ref[...] = h.T.astype(BF16)
    q = jnp.dot(h.astype(BF16), wq_ref[...], preferred_element_type=F32)
    for hh in range(2 * PEER_HEADS):
        q_scr[hh] = q[:, hh * LANES:(hh + 1) * LANES].astype(BF16)

    def head(hd, carry):
        s1 = lax.dot_general(k1_ref[hd], q_scr[2 * hd], _NT, preferred_element_type=F32)
        s2 = lax.dot_general(k2_ref[hd], q_scr[2 * hd + 1], _NT, preferred_element_type=F32)
        v1 = _top_values(s1, PEER_TOPK)
        v2 = _top_values(s2, PEER_TOPK)
        v2all = jnp.concatenate(v2, axis=0)
        cand = jnp.concatenate([v1[a] + v2all for a in range(PEER_TOPK)], axis=0)
        thr = _top_values(cand, PEER_TOPK)[-1]
        top = v1[0] + v2[0]
        z = jnp.sum(jnp.where(cand >= thr, jnp.exp(cand - top), 0.0), axis=0, keepdims=True)
        s1_ref[hd] = s1
        s2_ref[hd] = s2
        e1_ref[hd] = jnp.exp(s1 - v1[0]) / z
        e2_ref[hd] = jnp.exp(s2 - v2[0])
        thr_ref[hd] = thr
        return carry

    lax.fori_loop(0, PEER_HEADS, head, 0)


def _peer_score(x, g, sc, sh, wq, k1, k2, seq, tt):
    t, d = x.shape
    bidx = lambda i: ((i * tt) // seq, 0, 0)
    full = lambda a: pl.BlockSpec(a.shape, lambda i: (0,) * a.ndim)
    hk = pl.BlockSpec((PEER_HEADS, PEER_NKEYS, tt), lambda i: (0, 0, i))
    hk_shape = jax.ShapeDtypeStruct((PEER_HEADS, PEER_NKEYS, t), F32)
    return pl.pallas_call(
        _peer_score_kernel,
        grid=(t // tt,),
        in_specs=[pl.BlockSpec((tt, d), lambda i: (i, 0)),
                  pl.BlockSpec((1, d), lambda i: (0, 0)),
                  pl.BlockSpec((1, 1, d), bidx),
                  pl.BlockSpec((1, 1, d), bidx),
                  full(wq), full(k1), full(k2)],
        out_specs=[pl.BlockSpec((d, tt), lambda i: (0, i)), hk, hk, hk, hk,
                   pl.BlockSpec((PEER_HEADS, 1, tt), lambda i: (0, 0, i))],
        out_shape=[jax.ShapeDtypeStruct((d, t), BF16), hk_shape, hk_shape, hk_shape, hk_shape,
                   jax.ShapeDtypeStruct((PEER_HEADS, 1, t), F32)],
        scratch_shapes=[pltpu.VMEM((2 * PEER_HEADS, tt, LANES), BF16)],
        compiler_params=pltpu.CompilerParams(dimension_semantics=("parallel",),
                                             vmem_limit_bytes=VMEM_LIMIT),
        name="peer_score",
    )(x, g, sc, sh, wq, k1, k2)


def _gelu(a):
    return 0.5 * a * (1.0 + lax.erf(a * (2.0 ** -0.5)))


def _peer_dense_kernel(ht_ref, s1_ref, s2_ref, e1_ref, e2_ref, thr_ref, u_ref, vt_ref,
                       x_ref, g2_ref, fg_ref, o_ref, acc_ref, gact_ref, *, te, tt, final):
    j = pl.program_id(1)

    @pl.when(j == 0)
    def _():
        acc_ref[...] = jnp.zeros_like(acc_ref)

    a = jnp.dot(u_ref[...], ht_ref[...], preferred_element_type=F32)
    rows_per_step = te // PEER_NKEYS
    irows = pl.ds(pl.multiple_of(j * rows_per_step, rows_per_step), rows_per_step)
    for ii in range(rows_per_step):
        for tc in range(tt // LANES):
            ts_ = slice(tc * LANES, (tc + 1) * LANES)
            w = jnp.zeros((PEER_NKEYS, LANES), F32)
            for hd in range(PEER_HEADS):
                s1row = s1_ref[hd, irows, ts_][ii:ii + 1]
                e1row = e1_ref[hd, irows, ts_][ii:ii + 1]
                sel = (s1row + s2_ref[hd, :, ts_]) >= thr_ref[hd, :, ts_]
                w = w + jnp.where(sel, e2_ref[hd, :, ts_], 0.0) * e1row
            act = _gelu(a[ii * PEER_NKEYS:(ii + 1) * PEER_NKEYS, ts_])
            gact_ref[ii * PEER_NKEYS:(ii + 1) * PEER_NKEYS, ts_] = (act * w).astype(BF16)
    acc_ref[...] += jnp.dot(vt_ref[...], gact_ref[...], preferred_element_type=F32)

    @pl.when(j == pl.num_programs(1) - 1)
    def _():
        y = x_ref[...] + g2_ref[0] * acc_ref[...].T
        if final:
            y = (y * lax.rsqrt(jnp.mean(y * y, axis=-1, keepdims=True) + EPS)) * fg_ref[...]
        o_ref[...] = y


def _peer_dense(ht, s1, s2, e1, e2, thr, u, vt, x, g2, fg, seq, tt, te, final):
    t, d = x.shape
    ne = u.shape[0]
    hk = pl.BlockSpec((PEER_HEADS, PEER_NKEYS, tt), lambda i, j: (0, 0, i))
    return pl.pallas_call(
        functools.partial(_peer_dense_kernel, te=te, tt=tt, final=final),
        grid=(t // tt, ne // te),
        in_specs=[pl.BlockSpec((d, tt), lambda i, j: (0, i)),
                  hk, hk, hk, hk,
                  pl.BlockSpec((PEER_HEADS, 1, tt), lambda i, j: (0, 0, i)),
                  pl.BlockSpec((te, d), lambda i, j: (j, 0)),
                  pl.BlockSpec((d, te), lambda i, j: (0, j)),
                  pl.BlockSpec((tt, d), lambda i, j: (i, 0)),
                  pl.BlockSpec((1, 1, d), lambda i, j: ((i * tt) // seq, 0, 0)),
                  pl.BlockSpec((1, d), lambda i, j: (0, 0))],
        out_specs=pl.BlockSpec((tt, d), lambda i, j: (i, 0)),
        out_shape=jax.ShapeDtypeStruct((t, d), F32),
        scratch_shapes=[pltpu.VMEM((d, tt), F32), pltpu.VMEM((te, tt), BF16)],
        compiler_params=pltpu.CompilerParams(dimension_semantics=("parallel", "arbitrary"),
                                             vmem_limit_bytes=VMEM_LIMIT),
        name="peer_dense",
    )(ht, s1, s2, e1, e2, thr, u, vt, x, g2, fg)


def _tile(n, want):
    while n % want:
        want //= 2
    return want


def kernel(x, c, positions, ada_w, ada_b, norm1_g, norm2_g, w_in, ret_gn_g, sinks, w_out,
           peer_wq, peer_k1, peer_k2, peer_u, peer_v, final_g):
    batch, seq, d = x.shape
    depth = ada_w.shape[0]
    t = batch * seq
    tm = _tile(seq, 256)
    ts = _tile(seq, 256)
    tt_score = _tile(seq, 256)
    tt_dense = _tile(seq, 512)
    te = SUBLANES * PEER_NKEYS

    mod = _adaln(c, ada_w, ada_b)
    tables = _rotary_tables(positions)
    decay, qwm, kwm, cdm = _retention_consts()
    msk = jnp.asarray(_slot_masks())
    col_perm = jnp.asarray(_in_col_perm())

    xf = x.reshape(t, d)
    for l in range(depth):
        sh1, sc1, g1, sh2, sc2, g2 = [mod[l, :, i * d:(i + 1) * d].reshape(batch, 1, d)
                                      for i in range(6)]
        w_in_l = jnp.take(w_in[l], col_perm, axis=1).astype(BF16)
        proj = _norm_proj(xf, norm1_g[l].reshape(1, d), sc1, sh1, w_in_l, IN_WIDTHS, seq, tm, F32)
        merged = _mixer(proj, tables, (decay, qwm, kwm, cdm, msk), sinks[l],
                        ret_gn_g[l].reshape(1, d), batch, seq, ts)
        xf = _outproj(merged, w_out[l].astype(BF16), xf, g1, seq, tm)
        ht, s1, s2, e1, e2, thr = _peer_score(
            xf, norm2_g[l].reshape(1, d), sc2, sh2, peer_wq[l].astype(BF16),
            peer_k1[l].astype(BF16), peer_k2[l].astype(BF16), seq, tt_score)
        xf = _peer_dense(ht, s1, s2, e1, e2, thr, peer_u[l].astype(BF16),
                         peer_v[l].T.astype(BF16), xf, g2, final_g.reshape(1, d),
                         seq, tt_dense, te, final=(l == depth - 1))
    return xf.reshape(batch, seq, d)
```

```python
import functools

import numpy as np
import jax
import jax.numpy as jnp
from jax import lax
from jax.experimental import pallas as pl
from jax.experimental.pallas import tpu as pltpu

F32 = jnp.float32
BF16 = jnp.bfloat16

D_MODEL = 1024
RET_HEADS = 8
RET_DK = 64
RET_DV = 128
RET_ROPE_BASE = 10000.0
ATT_HD = 64
ATT_HEADS = 16
ATT_KV_HEADS = 4
ROPE_DIMS = 16
ROPE_THETA = 500000.0
BLOCK = 128
PEER_HEADS = 8
PEER_NKEYS = 128
PEER_TOPK = 16
PEER_QDIM = 256
EPS = 1e-6
NEG_INF = -1e30
LANES = 128
SUBLANES = 8
SUB_EXPERTS = 256
CHUNK = 256
STAGES_PER_TRIP = 4
VMEM_LIMIT = 56 * 1024 * 1024

IN_WIDTHS = (512, 512, 1024, 1024, 1024, 256, 256, 1024, 1024)

_NT = (((1,), (1,)), ((), ()))
_TN = (((0,), (0,)), ((), ()))


def _pair_lane(slot, pos):
    return (pos // 32) * 64 + slot * 32 + pos % 32


def _pair_perm(n_heads, dim_order):
    src = np.zeros(n_heads * 64, np.int32)
    for p in range(n_heads // 2):
        for slot in range(2):
            for pos in range(64):
                src[p * LANES + _pair_lane(slot, pos)] = (2 * p + slot) * 64 + dim_order[pos]
    return src


def _att_dim_order():
    return np.concatenate([np.arange(0, 8), np.arange(16, 40), np.arange(8, 16), np.arange(40, 64)])


def _in_col_perm():
    ret = _pair_perm(RET_HEADS, np.arange(64))
    att_q = _pair_perm(ATT_HEADS, _att_dim_order())
    att_k = _pair_perm(ATT_KV_HEADS, _att_dim_order())
    starts = np.concatenate([[0], np.cumsum(IN_WIDTHS)[:-1]])
    cols = np.arange(sum(IN_WIDTHS), dtype=np.int32)
    cols[starts[0]:starts[0] + 512] = starts[0] + ret
    cols[starts[1]:starts[1] + 512] = starts[1] + ret
    cols[starts[4]:starts[4] + 1024] = starts[4] + att_q
    cols[starts[5]:starts[5] + 256] = starts[5] + att_k
    return cols


def _slot_masks():
    lane = np.arange(LANES)
    slot = (lane // 32) % 2
    return np.stack([(slot == 0), (slot == 1)]).astype(np.float32)


def _retention_consts():
    h = np.arange(RET_HEADS, dtype=np.float64)
    log_g = np.log(1.0 - np.power(2.0, -5.0 - h))
    idx = np.arange(BLOCK, dtype=np.float64)
    diff = idx[:, None] - idx[None, :]
    decay = np.where(diff >= 0, np.exp(log_g[:, None, None] * np.maximum(diff, 0.0)), 0.0)
    k_w = np.exp(log_g[:, None] * (BLOCK - 1.0 - idx)[None, :])
    q_w = np.exp(log_g[:, None] * (idx + 1.0)[None, :])
    cdec = np.exp(log_g * BLOCK)
    masks = _slot_masks().astype(np.float64)
    qwm = np.stack([q_w[hh][:, None] * masks[hh % 2][None, :] for hh in range(RET_HEADS)])
    kwm = np.stack([k_w[hh][:, None] * masks[hh % 2][None, :] * (RET_DK ** -0.5)
                    for hh in range(RET_HEADS)])
    row_slot = (np.arange(LANES) // 32) % 2
    cdm = np.stack([np.broadcast_to(cdec[2 * p + row_slot][:, None], (LANES, LANES))
                    for p in range(RET_HEADS // 2)])
    f = lambda a: jnp.asarray(a.astype(np.float32))
    return f(decay), f(qwm), f(kwm), f(cdm)


def _rotary_tables(positions):
    pos = positions.reshape(-1).astype(F32)
    lane = np.arange(LANES)
    hpos = (lane // 64) * 32 + lane % 32
    sign = np.where(lane < 64, -1.0, 1.0).astype(np.float32)
    half = RET_DK // 2
    inv_r = jnp.power(RET_ROPE_BASE, -jnp.arange(half, dtype=F32) * (2.0 / RET_DK))
    ang_r = pos[:, None] * inv_r[None, :]
    fr = (hpos % 32)
    cr = jnp.cos(ang_r)[:, fr]
    sr = jnp.sin(ang_r)[:, fr] * sign[None, :]
    half_a = ROPE_DIMS // 2
    inv_a = jnp.power(ROPE_THETA, -jnp.arange(half_a, dtype=F32) * (2.0 / ROPE_DIMS))
    ang_a = pos[:, None] * inv_a[None, :]
    is_rot = (hpos % 32) < half_a
    fa = np.where(is_rot, hpos % 32, 0)
    ca = jnp.where(is_rot[None, :], jnp.cos(ang_a)[:, fa], 1.0)
    sa = jnp.where(is_rot[None, :], jnp.sin(ang_a)[:, fa] * sign[None, :], 0.0)
    return cr, sr, ca, sa


def _adaln_kernel(c_ref, w_ref, b_ref, o_ref):
    c = c_ref[...]
    cond = c * jax.nn.sigmoid(c)
    o_ref[0] = jnp.dot(cond, w_ref[0], preferred_element_type=F32,
                       precision=lax.Precision.HIGHEST) + b_ref[0]


def _adaln(c, ada_w, ada_b):
    depth, d, n = ada_w.shape
    b = c.shape[0]
    return pl.pallas_call(
        _adaln_kernel,
        grid=(depth, n // d),
        in_specs=[pl.BlockSpec((b, d), lambda l, j: (0, 0)),
                  pl.BlockSpec((1, d, d), lambda l, j: (l, 0, j)),
                  pl.BlockSpec((1, 1, d), lambda l, j: (l, 0, j))],
        out_specs=pl.BlockSpec((1, b, d), lambda l, j: (l, 0, j)),
        out_shape=jax.ShapeDtypeStruct((depth, b, n), F32),
        name="adaln",
    )(c, ada_w, ada_b.reshape(depth, 1, n))


def _modulated_norm(x, g, sc, sh):
    y = x * lax.rsqrt(jnp.mean(x * x, axis=-1, keepdims=True) + EPS)
    return (y * g) * (1.0 + sc) + sh


def _proj_kernel(x_ref, g_ref, sc_ref, sh_ref, w_ref, *out_refs, widths):
    hb = _modulated_norm(x_ref[...], g_ref[...], sc_ref[0], sh_ref[0]).astype(BF16)
    off = 0
    for o_ref, wd in zip(out_refs, widths):
        o_ref[...] = jnp.dot(hb, w_ref[:, off:off + wd],
                             preferred_element_type=F32).astype(o_ref.dtype)
        off += wd


def _norm_proj(x, g, sc, sh, w, widths, seq, tm, out_dtype):
    t, d = x.shape
    n = w.shape[1]
    bidx = lambda i: ((i * tm) // seq, 0, 0)
    return pl.pallas_call(
        functools.partial(_proj_kernel, widths=widths),
        grid=(t // tm,),
        in_specs=[pl.BlockSpec((tm, d), lambda i: (i, 0)),
                  pl.BlockSpec((1, d), lambda i: (0, 0)),
                  pl.BlockSpec((1, 1, d), bidx),
                  pl.BlockSpec((1, 1, d), bidx),
                  pl.BlockSpec((d, n), lambda i: (0, 0))],
        out_specs=[pl.BlockSpec((tm, wd), lambda i: (i, 0)) for wd in widths],
        out_shape=[jax.ShapeDtypeStruct((t, wd), out_dtype) for wd in widths],
        compiler_params=pltpu.CompilerParams(dimension_semantics=("parallel",),
                                             vmem_limit_bytes=VMEM_LIMIT),
        name="norm_proj",
    )(x, g, sc, sh, w)


def _rot(t, c, s):
    return t * c + pltpu.roll(t, 64, 1) * s


def _mixer_kernel(sinks_ref, rq_ref, rk_ref, rv_ref, rg_ref, aq_ref, ak_ref, av_ref,
                  ga_ref, gb_ref, cr_ref, sr_ref, ca_ref, sa_ref,
                  decay_ref, qwm_ref, kwm_ref, cdm_ref, msk_ref, gn_ref,
                  out_ref, state_ref, pk_ref, pv_ref, *, n_chunks):
    step = pl.program_id(1)

    @pl.when(step == 0)
    def _():
        state_ref[...] = jnp.zeros_like(state_ref)
        pk_ref[...] = jnp.zeros_like(pk_ref)
        pv_ref[...] = jnp.zeros_like(pv_ref)

    def chunk(c, carry):
        rows = pl.ds(pl.multiple_of(c * BLOCK, BLOCK), BLOCK)
        lane = lax.broadcasted_iota(jnp.int32, (BLOCK, LANES), 1)
        slot_is = [((lane // 32) % 2) == 0, ((lane // 32) % 2) == 1]
        low_half = lane < 64
        m_slot = [msk_ref[0:1, :], msk_ref[1:2, :]]
        row = lax.broadcasted_iota(jnp.int32, (BLOCK, 2 * BLOCK), 0)
        col = lax.broadcasted_iota(jnp.int32, (BLOCK, 2 * BLOCK), 1)
        first_key = jnp.where(jnp.logical_and(step == 0, c == 0), BLOCK, 0)
        lo = jnp.maximum(row + 1, first_key)
        valid = jnp.logical_and(col >= lo, col <= row + BLOCK)
        cr = cr_ref[rows, :]
        sr = sr_ref[rows, :]
        ca = ca_ref[rows, :]
        sa = sa_ref[rows, :]

        ya = []
        for p in range(RET_HEADS // 2):
            cs = slice(p * LANES, (p + 1) * LANES)
            q = _rot(rq_ref[rows, cs], cr, sr)
            k = _rot(rk_ref[rows, cs], cr, sr)
            qs = jnp.concatenate([q * m_slot[0], q * m_slot[1]], axis=0).astype(BF16)
            kb = (k * (RET_DK ** -0.5)).astype(BF16)
            sc = lax.dot_general(qs, kb, _NT, preferred_element_type=F32)
            st = state_ref[p]
            stb = st.astype(BF16)
            vbs = []
            for s in range(2):
                h = 2 * p + s
                hs = slice(h * RET_DV, (h + 1) * RET_DV)
                vb = rv_ref[rows, hs].astype(BF16)
                vbs.append(vb)
                sch = sc[s * BLOCK:(s + 1) * BLOCK] * decay_ref[h]
                qc = q * qwm_ref[h]
                lhs = jnp.concatenate([sch, qc], axis=1).astype(BF16)
                rhs = jnp.concatenate([vb, stb], axis=0)
                o = jnp.dot(lhs, rhs, preferred_element_type=F32)
                mu = jnp.mean(o, axis=-1, keepdims=True)
                dlt = o - mu
                var = jnp.mean(dlt * dlt, axis=-1, keepdims=True)
                y = (dlt * lax.rsqrt(var + EPS)) * gn_ref[:, hs]
                rg = rg_ref[rows, hs]
                ya.append(y * (rg * jax.nn.sigmoid(rg)))
            kw = jnp.concatenate([k * kwm_ref[2 * p], k * kwm_ref[2 * p + 1]], axis=0).astype(BF16)
            vst = jnp.concatenate(vbs, axis=0)
            upd = lax.dot_general(kw, vst, _TN, preferred_element_type=F32)
            state_ref[p] = st * cdm_ref[p] + upd

        yb = [None] * (ATT_HEADS // 2)
        kps, vps = [], []
        for pp in range(ATT_KV_HEADS // 2):
            cs = slice(pp * LANES, (pp + 1) * LANES)
            kps.append(_rot(ak_ref[rows, cs], ca, sa))
            vps.append(av_ref[rows, cs])
        for g in range(ATT_KV_HEADS):
            kp, vp = kps[g // 2], vps[g // 2]
            if g % 2 == 0:
                kd = jnp.where(slot_is[0], kp, pltpu.roll(kp, 32, 1))
                vd = jnp.where(low_half, vp, pltpu.roll(vp, 64, 1))
            else:
                kd = jnp.where(slot_is[1], kp, pltpu.roll(kp, 96, 1))
                vd = jnp.where(low_half, pltpu.roll(vp, 64, 1), vp)
            kdb = kd.astype(BF16)
            vdb = vd.astype(BF16)
            kcat = jnp.concatenate([pk_ref[g], kdb], axis=0)
            vcat = jnp.concatenate([pv_ref[g], vdb], axis=0)
            pk_ref[g] = kdb
            pv_ref[g] = vdb
            qparts = []
            for tt in range(2):
                t = 2 * g + tt
                qt = _rot(aq_ref[rows, t * LANES:(t + 1) * LANES], ca, sa)
                for s in range(2):
                    qparts.append(qt * (m_slot[s] * (ATT_HD ** -0.5)))
            qs = jnp.concatenate(qparts, axis=0).astype(BF16)
            s_all = lax.dot_general(qs, kcat, _NT, preferred_element_type=F32)
            pparts = []
            for r in range(4):
                sr_ = jnp.where(valid, s_all[r * BLOCK:(r + 1) * BLOCK], NEG_INF)
                sink = sinks_ref[4 * g + r]
                m = jnp.maximum(jnp.max(sr_, axis=-1, keepdims=True), sink)
                pe = jnp.exp(sr_ - m)
                den = jnp.sum(pe, axis=-1, keepdims=True) + jnp.exp(sink - m)
                pparts.append(pe / den)
            pb = jnp.concatenate(pparts, axis=0).astype(BF16)
            o = jnp.dot(pb, vcat, preferred_element_type=F32)
            for tt in range(2):
                yb[2 * g + tt] = jnp.where(low_half, o[(2 * tt) * BLOCK:(2 * tt + 1) * BLOCK],
                                           o[(2 * tt + 1) * BLOCK:(2 * tt + 2) * BLOCK])

        for ct in range(D_MODEL // LANES):
            cs = slice(ct * LANES, (ct + 1) * LANES)
            mrg = jax.nn.sigmoid(ga_ref[rows, cs]) * ya[ct] + jax.nn.sigmoid(gb_ref[rows, cs]) * yb[ct]
            out_ref[rows, cs] = mrg.astype(out_ref.dtype)
        return carry

    lax.fori_loop(0, n_chunks, chunk, 0)


def _mixer(proj, tables, consts, sinks, gn, batch, seq, ts):
    rq, rk, rv, rg, aq, ak, av, ga, gb = proj
    cr, sr, ca, sa = tables
    decay, qwm, kwm, cdm, msk = consts
    t = rq.shape[0]
    nsteps = seq // ts
    row_spec = lambda w: pl.BlockSpec((ts, w), lambda b, n: (b * nsteps + n, 0))
    full = lambda a: pl.BlockSpec(a.shape, lambda b, n: (0,) * a.ndim)
    return pl.pallas_call(
        functools.partial(_mixer_kernel, n_chunks=ts // BLOCK),
        grid=(batch, nsteps),
        in_specs=[pl.BlockSpec(memory_space=pltpu.SMEM)]
        + [row_spec(a.shape[1]) for a in (rq, rk, rv, rg, aq, ak, av, ga, gb, cr, sr, ca, sa)]
        + [full(a) for a in (decay, qwm, kwm, cdm, msk, gn)],
        out_specs=row_spec(D_MODEL),
        out_shape=jax.ShapeDtypeStruct((t, D_MODEL), BF16),
        scratch_shapes=[pltpu.VMEM((RET_HEADS // 2, LANES, RET_DV), F32),
                        pltpu.VMEM((ATT_KV_HEADS, BLOCK, LANES), BF16),
                        pltpu.VMEM((ATT_KV_HEADS, BLOCK, LANES), BF16)],
        compiler_params=pltpu.CompilerParams(dimension_semantics=("parallel", "arbitrary"),
                                             vmem_limit_bytes=VMEM_LIMIT),
        name="mixer",
    )(sinks, rq, rk, rv, rg, aq, ak, av, ga, gb, cr, sr, ca, sa, decay, qwm, kwm, cdm, msk, gn)


def _outproj_kernel(m_ref, w_ref, x_ref, g_ref, o_ref):
    o_ref[...] = x_ref[...] + g_ref[0] * jnp.dot(m_ref[...], w_ref[...],
                                                 preferred_element_type=F32)


def _outproj(merged, w, x, gate, seq, tm):
    t, d = x.shape
    return pl.pallas_call(
        _outproj_kernel,
        grid=(t // tm,),
        in_specs=[pl.BlockSpec((tm, d), lambda i: (i, 0)),
                  pl.BlockSpec((d, d), lambda i: (0, 0)),
                  pl.BlockSpec((tm, d), lambda i: (i, 0)),
                  pl.BlockSpec((1, 1, d), lambda i: ((i * tm) // seq, 0, 0))],
        out_specs=pl.BlockSpec((tm, d), lambda i: (i, 0)),
        out_shape=jax.ShapeDtypeStruct((t, d), F32),
        compiler_params=pltpu.CompilerParams(dimension_semantics=("parallel",),
                                             vmem_limit_bytes=VMEM_LIMIT),
        name="outproj",
    )(merged, w, x, gate)


def _oe_sort_pairs(n):
    pairs = []
    p = 1
    while p < n:
        k = p
        while k >= 1:
            for j in range(k % p, n - k, 2 * k):
                for i in range(min(k, n - j - k)):
                    if (i + j) // (2 * p) == (i + j + k) // (2 * p):
                        pairs.append((i + j, i + j + k))
            k //= 2
        p *= 2
    return pairs


def _bitonic_merge_pairs(n):
    pairs = []
    k = n // 2
    while k >= 1:
        pairs += [(i, i + k) for i in range(n) if (i & k) == 0]
        k //= 2
    return pairs


def _compare_exchange(x, pairs):
    for i, j in pairs:
        hi = jnp.maximum(x[i], x[j])
        lo = jnp.minimum(x[i], x[j])
        x[i], x[j] = hi, lo
    return x


def _all_sublanes(x, op):
    for sh in (4, 2, 1):
        x = op(x, pltpu.roll(x, sh, 0))
    return x


def _sorted_top16(blocks):
    x = _compare_exchange(list(blocks), _oe_sort_pairs(PEER_TOPK))
    for sh in (4, 2, 1):
        y = [pltpu.roll(v, sh, 0) for v in x]
        x = [jnp.maximum(x[k], y[PEER_TOPK - 1 - k]) for k in range(PEER_TOPK)]
        x = _compare_exchange(x, _bitonic_merge_pairs(PEER_TOPK))
    return x


def _next_largest(blocks, v_last):
    m = None
    for b in blocks:
        c = jnp.where(b < v_last, b, -jnp.inf)
        m = c if m is None else jnp.maximum(m, c)
    return _all_sublanes(m, jnp.maximum)


def _by_sublane(vals, sub):
    out = vals[SUBLANES - 1]
    for r in range(SUBLANES - 1):
        out = jnp.where(sub == r, vals[r], out)
    return out


def _peer_score_kernel(x_ref, g_ref, sc_ref, sh_ref, wq_ref, k1_ref, k2_ref,
                       ht_ref, tau_ref, e1_ref, e2_ref, q_scr):
    h = _modulated_norm(x_ref[...], g_ref[...], sc_ref[0], sh_ref[0])
    ht_ref[0] = h.T.astype(BF16)
    q = jnp.dot(h.astype(BF16), wq_ref[...], preferred_element_type=F32)
    for hh in range(2 * PEER_HEADS):
        q_scr[hh] = q[:, hh * LANES:(hh + 1) * LANES].astype(BF16)

    def head(hd, carry):
        s1 = lax.dot_general(k1_ref[hd], q_scr[2 * hd], _NT, preferred_element_type=F32)
        s2 = lax.dot_general(k2_ref[hd], q_scr[2 * hd + 1], _NT, preferred_element_type=F32)
        nb = PEER_NKEYS // SUBLANES
        b1 = [s1[SUBLANES * k:SUBLANES * (k + 1)] for k in range(nb)]
        b2 = [s2[SUBLANES * k:SUBLANES * (k + 1)] for k in range(nb)]
        v1 = _sorted_top16(b1)
        v2 = _sorted_top16(b2)
        sub = lax.broadcasted_iota(jnp.int32, v1[0].shape, 0)
        v2lo = _by_sublane(v2[:SUBLANES], sub)
        v2hi = _by_sublane(v2[SUBLANES:], sub)
        v1hi = _by_sublane(v1[SUBLANES:], sub)
        cands = [v1[0] + v2lo, v1[0] + v2hi] + [v1[a] + v2lo for a in range(1, SUBLANES)] \
            + [v1hi + v2[0]]
        work = list(cands)
        for r in range(PEER_TOPK):
            m = work[0]
            for c in work[1:]:
                m = jnp.maximum(m, c)
            m = _all_sublanes(m, jnp.maximum)
            work = [jnp.where(c == m, -jnp.inf, c) for c in work]
        c16 = m
        m = work[0]
        for c in work[1:]:
            m = jnp.maximum(m, c)
        c17 = jnp.maximum(_all_sublanes(m, jnp.maximum),
                          jnp.maximum(_next_largest(b1, v1[-1]) + v2[0],
                                      v1[0] + _next_largest(b2, v2[-1])))
        thr = 0.5 * (c16 + c17)
        top = v1[0] + v2[0]
        z = None
        for c in cands:
            e = jnp.where(c >= thr, jnp.exp(c - top), 0.0)
            z = e if z is None else z + e
        z = _all_sublanes(z, jnp.add)
        scale = 0.5 / z
        tau_ref[0, hd] = jnp.concatenate([jnp.exp((thr - v2[0]) - b) for b in b1], axis=0)
        e1_ref[0, hd] = jnp.concatenate([jnp.exp(b - v1[0]) * scale for b in b1], axis=0)
        e2_ref[0, hd] = jnp.concatenate([jnp.exp(b - v2[0]) for b in b2], axis=0)
        return carry

    lax.fori_loop(0, PEER_HEADS, head, 0)


def _peer_score(x, g, sc, sh, wq, k1, k2, seq, tt):
    t, d = x.shape
    bidx = lambda i: ((i * tt) // seq, 0, 0)
    full = lambda a: pl.BlockSpec(a.shape, lambda i: (0,) * a.ndim)
    hk = pl.BlockSpec((1, PEER_HEADS, PEER_NKEYS, tt), lambda i: (i, 0, 0, 0))
    hk_shape = jax.ShapeDtypeStruct((t // tt, PEER_HEADS, PEER_NKEYS, tt), F32)
    return pl.pallas_call(
        _peer_score_kernel,
        grid=(t // tt,),
        in_specs=[pl.BlockSpec((tt, d), lambda i: (i, 0)),
                  pl.BlockSpec((1, d), lambda i: (0, 0)),
                  pl.BlockSpec((1, 1, d), bidx),
                  pl.BlockSpec((1, 1, d), bidx),
                  full(wq), full(k1), full(k2)],
        out_specs=[pl.BlockSpec((1, d, tt), lambda i: (i, 0, 0)), hk, hk, hk],
        out_shape=[jax.ShapeDtypeStruct((t // tt, d, tt), BF16), hk_shape, hk_shape, hk_shape],
        scratch_shapes=[pltpu.VMEM((2 * PEER_HEADS, tt, LANES), BF16)],
        compiler_params=pltpu.CompilerParams(dimension_semantics=("parallel",),
                                             vmem_limit_bytes=VMEM_LIMIT),
        name="peer_score",
    )(x, g, sc, sh, wq, k1, k2)


def _peer_dense_kernel(ht_ref, tau_ref, e1_ref, e2_ref, u_ref, vt_ref,
                       x_ref, g2_ref, fg_ref, o_ref,
                       acc_ref, a0_scr, a1_scr, g0_scr, g1_scr, *, nsub, nchunk, final):
    j = pl.program_id(1)

    @pl.when(j == 0)
    def _():
        acc_ref[...] = jnp.zeros_like(acc_ref)

    keys_per_sub = SUB_EXPERTS // PEER_NKEYS
    nstage = nsub * nchunk

    def stage(s, a_cur, a_nxt, g_cur, g_prv):
        sp = jnp.maximum(s - 1, 0)
        cp = sp % nchunk
        acc_ref[cp] += jnp.dot(vt_ref[sp // nchunk], g_prv[...], preferred_element_type=F32)
        sn = jnp.minimum(s + 1, nstage - 1)
        urows = pl.ds(pl.multiple_of((sn // nchunk) * SUB_EXPERTS, SUB_EXPERTS), SUB_EXPERTS)
        a_nxt[...] = jnp.dot(u_ref[urows, :], ht_ref[sn % nchunk], preferred_element_type=F32)
        k = s // nchunk
        c = s % nchunk
        key0 = (j * nsub + k) * keys_per_sub
        grp = pl.ds(pl.multiple_of((key0 // SUBLANES) * SUBLANES, SUBLANES), SUBLANES)
        shift = (SUBLANES - key0 % SUBLANES) % SUBLANES
        tau_g = [pltpu.roll(tau_ref[c, hd, grp, :], shift, 0) for hd in range(PEER_HEADS)]
        e1_g = [pltpu.roll(e1_ref[c, hd, grp, :], shift, 0) for hd in range(PEER_HEADS)]
        for tc in range(CHUNK // LANES):
            ts_ = slice(tc * LANES, (tc + 1) * LANES)
            w = [None] * keys_per_sub
            for hd in range(PEER_HEADS):
                e2v = e2_ref[c, hd, :, ts_]
                for r in range(keys_per_sub):
                    term = jnp.where(e2v >= tau_g[hd][r:r + 1, ts_], e2v, 0.0) * e1_g[hd][r:r + 1, ts_]
                    w[r] = term if w[r] is None else w[r] + term
            for r in range(keys_per_sub):
                es = slice(r * PEER_NKEYS, (r + 1) * PEER_NKEYS)
                av = a_cur[es, ts_]
                act = av * (1.0 + lax.erf(av * (2.0 ** -0.5)))
                g_cur[es, ts_] = (act * w[r]).astype(BF16)

    a0_scr[...] = jnp.dot(u_ref[0:SUB_EXPERTS, :], ht_ref[0], preferred_element_type=F32)
    g1_scr[...] = jnp.zeros_like(g1_scr)

    def body(m, carry):
        for i in range(0, STAGES_PER_TRIP, 2):
            stage(STAGES_PER_TRIP * m + i, a0_scr, a1_scr, g0_scr, g1_scr)
            stage(STAGES_PER_TRIP * m + i + 1, a1_scr, a0_scr, g1_scr, g0_scr)
        return carry

    lax.fori_loop(0, nstage // STAGES_PER_TRIP, body, 0)
    acc_ref[nchunk - 1] += jnp.dot(vt_ref[nsub - 1], g1_scr[...], preferred_element_type=F32)

    @pl.when(j == pl.num_programs(1) - 1)
    def _():
        for c in range(nchunk):
            rows = slice(c * CHUNK, (c + 1) * CHUNK)
            y = x_ref[rows, :] + g2_ref[0] * acc_ref[c].T
            if final:
                y = (y * lax.rsqrt(jnp.mean(y * y, axis=-1, keepdims=True) + EPS)) * fg_ref[...]
            o_ref[rows, :] = y


def _peer_dense(ht, tau, e1, e2, u, vt, x, g2, fg, seq, tt, te, final):
    t, d = x.shape
    ne = u.shape[0]
    nsub = te // SUB_EXPERTS
    nchunk = tt // CHUNK
    hk = pl.BlockSpec((nchunk, PEER_HEADS, PEER_NKEYS, CHUNK), lambda i, j: (i, 0, 0, 0))
    return pl.pallas_call(
        functools.partial(_peer_dense_kernel, nsub=nsub, nchunk=nchunk, final=final),
        grid=(t // tt, ne // te),
        in_specs=[pl.BlockSpec((nchunk, d, CHUNK), lambda i, j: (i, 0, 0)),
                  hk, hk, hk,
                  pl.BlockSpec((te, d), lambda i, j: (j, 0)),
                  pl.BlockSpec((nsub, d, SUB_EXPERTS), lambda i, j: (j, 0, 0)),
                  pl.BlockSpec((tt, d), lambda i, j: (i, 0)),
                  pl.BlockSpec((1, 1, d), lambda i, j: ((i * tt) // seq, 0, 0)),
                  pl.BlockSpec((1, d), lambda i, j: (0, 0))],
        out_specs=pl.BlockSpec((tt, d), lambda i, j: (i, 0)),
        out_shape=jax.ShapeDtypeStruct((t, d), F32),
        scratch_shapes=[pltpu.VMEM((nchunk, d, CHUNK), F32),
                        pltpu.VMEM((SUB_EXPERTS, CHUNK), F32), pltpu.VMEM((SUB_EXPERTS, CHUNK), F32),
                        pltpu.VMEM((SUB_EXPERTS, CHUNK), BF16), pltpu.VMEM((SUB_EXPERTS, CHUNK), BF16)],
        compiler_params=pltpu.CompilerParams(dimension_semantics=("parallel", "arbitrary"),
                                             vmem_limit_bytes=VMEM_LIMIT),
        name="peer_dense",
    )(ht, tau, e1, e2, u, vt, x, g2, fg)


def _sub_block_t(v):
    ne, d = v.shape
    return v.astype(BF16).reshape(ne // SUB_EXPERTS, SUB_EXPERTS, d).transpose(0, 2, 1)


def _tile(n, want):
    while n % want:
        want //= 2
    return want


def kernel(x, c, positions, ada_w, ada_b, norm1_g, norm2_g, w_in, ret_gn_g, sinks, w_out,
           peer_wq, peer_k1, peer_k2, peer_u, peer_v, final_g):
    batch, seq, d = x.shape
    depth = ada_w.shape[0]
    t = batch * seq
    tm = _tile(seq, 256)
    ts = _tile(seq, 256)
    tt_score = CHUNK
    tt_dense = _tile(seq, 512)
    te = SUBLANES * PEER_NKEYS

    mod = _adaln(c, ada_w, ada_b)
    tables = _rotary_tables(positions)
    decay, qwm, kwm, cdm = _retention_consts()
    msk = jnp.asarray(_slot_masks())
    col_perm = jnp.asarray(_in_col_perm())

    xf = x.reshape(t, d)
    for l in range(depth):
        sh1, sc1, g1, sh2, sc2, g2 = [mod[l, :, i * d:(i + 1) * d].reshape(batch, 1, d)
                                      for i in range(6)]
        w_in_l = jnp.take(w_in[l], col_perm, axis=1).astype(BF16)
        proj = _norm_proj(xf, norm1_g[l].reshape(1, d), sc1, sh1, w_in_l, IN_WIDTHS, seq, tm, F32)
        merged = _mixer(proj, tables, (decay, qwm, kwm, cdm, msk), sinks[l],
                        ret_gn_g[l].reshape(1, d), batch, seq, ts)
        xf = _outproj(merged, w_out[l].astype(BF16), xf, g1, seq, tm)
        ht, tau, e1, e2 = _peer_score(
            xf, norm2_g[l].reshape(1, d), sc2, sh2, peer_wq[l].astype(BF16),
            peer_k1[l].astype(BF16), peer_k2[l].astype(BF16), seq, tt_score)
        xf = _peer_dense(ht, tau, e1, e2, peer_u[l].astype(BF16),
                         _sub_block_t(peer_v[l]), xf, g2, final_g.reshape(1, d),
                         seq, tt_dense, te, final=(l == depth - 1))
    return xf.reshape(batch, seq, d)
```

```python
import functools

import numpy as np
import jax
import jax.numpy as jnp
from jax import lax
from jax.experimental import pallas as pl
from jax.experimental.pallas import tpu as pltpu

F32 = jnp.float32
BF16 = jnp.bfloat16

D_MODEL = 1024
RET_HEADS = 8
RET_DK = 64
RET_DV = 128
RET_ROPE_BASE = 10000.0
ATT_HD = 64
ATT_HEADS = 16
ATT_KV_HEADS = 4
ROPE_DIMS = 16
ROPE_THETA = 500000.0
BLOCK = 128
PEER_HEADS = 8
PEER_NKEYS = 128
PEER_TOPK = 16
PEER_QDIM = 256
EPS = 1e-6
NEG_INF = -1e30
E2_FLOOR = 1e-30
BF16_STEP = 1 << 16
LANES = 128
SUBLANES = 8
SUB_EXPERTS = 256
CHUNK = 256
STAGES_PER_TRIP = 4
VMEM_LIMIT = 56 * 1024 * 1024

IN_WIDTHS = (512, 512, 1024, 1024, 1024, 256, 256, 1024, 1024)

_NT = (((1,), (1,)), ((), ()))
_TN = (((0,), (0,)), ((), ()))


def _pair_lane(slot, pos):
    return (pos // 32) * 64 + slot * 32 + pos % 32


def _pair_perm(n_heads, dim_order):
    src = np.zeros(n_heads * 64, np.int32)
    for p in range(n_heads // 2):
        for slot in range(2):
            for pos in range(64):
                src[p * LANES + _pair_lane(slot, pos)] = (2 * p + slot) * 64 + dim_order[pos]
    return src


def _att_dim_order():
    return np.concatenate([np.arange(0, 8), np.arange(16, 40), np.arange(8, 16), np.arange(40, 64)])


def _in_col_perm():
    ret = _pair_perm(RET_HEADS, np.arange(64))
    att_q = _pair_perm(ATT_HEADS, _att_dim_order())
    att_k = _pair_perm(ATT_KV_HEADS, _att_dim_order())
    starts = np.concatenate([[0], np.cumsum(IN_WIDTHS)[:-1]])
    cols = np.arange(sum(IN_WIDTHS), dtype=np.int32)
    cols[starts[0]:starts[0] + 512] = starts[0] + ret
    cols[starts[1]:starts[1] + 512] = starts[1] + ret
    cols[starts[4]:starts[4] + 1024] = starts[4] + att_q
    cols[starts[5]:starts[5] + 256] = starts[5] + att_k
    return cols


def _slot_masks():
    lane = np.arange(LANES)
    slot = (lane // 32) % 2
    return np.stack([(slot == 0), (slot == 1)]).astype(np.float32)


def _retention_consts():
    h = np.arange(RET_HEADS, dtype=np.float64)
    log_g = np.log(1.0 - np.power(2.0, -5.0 - h))
    idx = np.arange(BLOCK, dtype=np.float64)
    diff = idx[:, None] - idx[None, :]
    decay = np.where(diff >= 0, np.exp(log_g[:, None, None] * np.maximum(diff, 0.0)), 0.0)
    k_w = np.exp(log_g[:, None] * (BLOCK - 1.0 - idx)[None, :])
    q_w = np.exp(log_g[:, None] * (idx + 1.0)[None, :])
    cdec = np.exp(log_g * BLOCK)
    masks = _slot_masks().astype(np.float64)
    qwm = np.stack([q_w[hh][:, None] * masks[hh % 2][None, :] for hh in range(RET_HEADS)])
    kwm = np.stack([k_w[hh][:, None] * masks[hh % 2][None, :] * (RET_DK ** -0.5)
                    for hh in range(RET_HEADS)])
    row_slot = (np.arange(LANES) // 32) % 2
    cdm = np.stack([np.broadcast_to(cdec[2 * p + row_slot][:, None], (LANES, LANES))
                    for p in range(RET_HEADS // 2)])
    f = lambda a: jnp.asarray(a.astype(np.float32))
    return f(decay), f(qwm), f(kwm), f(cdm)


def _rotary_tables(positions):
    pos = positions.reshape(-1).astype(F32)
    lane = np.arange(LANES)
    hpos = (lane // 64) * 32 + lane % 32
    sign = np.where(lane < 64, -1.0, 1.0).astype(np.float32)
    half = RET_DK // 2
    inv_r = jnp.power(RET_ROPE_BASE, -jnp.arange(half, dtype=F32) * (2.0 / RET_DK))
    ang_r = pos[:, None] * inv_r[None, :]
    fr = (hpos % 32)
    cr = jnp.cos(ang_r)[:, fr]
    sr = jnp.sin(ang_r)[:, fr] * sign[None, :]
    half_a = ROPE_DIMS // 2
    inv_a = jnp.power(ROPE_THETA, -jnp.arange(half_a, dtype=F32) * (2.0 / ROPE_DIMS))
    ang_a = pos[:, None] * inv_a[None, :]
    is_rot = (hpos % 32) < half_a
    fa = np.where(is_rot, hpos % 32, 0)
    ca = jnp.where(is_rot[None, :], jnp.cos(ang_a)[:, fa], 1.0)
    sa = jnp.where(is_rot[None, :], jnp.sin(ang_a)[:, fa] * sign[None, :], 0.0)
    return cr, sr, ca, sa


def _adaln_kernel(c_ref, w_ref, b_ref, o_ref):
    c = c_ref[...]
    cond = c * jax.nn.sigmoid(c)
    o_ref[0] = jnp.dot(cond, w_ref[0], preferred_element_type=F32,
                       precision=lax.Precision.HIGHEST) + b_ref[0]


def _adaln(c, ada_w, ada_b):
    depth, d, n = ada_w.shape
    b = c.shape[0]
    return pl.pallas_call(
        _adaln_kernel,
        grid=(depth, n // d),
        in_specs=[pl.BlockSpec((b, d), lambda l, j: (0, 0)),
                  pl.BlockSpec((1, d, d), lambda l, j: (l, 0, j)),
                  pl.BlockSpec((1, 1, d), lambda l, j: (l, 0, j))],
        out_specs=pl.BlockSpec((1, b, d), lambda l, j: (l, 0, j)),
        out_shape=jax.ShapeDtypeStruct((depth, b, n), F32),
        name="adaln",
    )(c, ada_w, ada_b.reshape(depth, 1, n))


def _modulated_norm(x, g, sc, sh):
    y = x * lax.rsqrt(jnp.mean(x * x, axis=-1, keepdims=True) + EPS)
    return (y * g) * (1.0 + sc) + sh


def _proj_kernel(x_ref, g_ref, sc_ref, sh_ref, w_ref, *out_refs, widths):
    hb = _modulated_norm(x_ref[...], g_ref[...], sc_ref[0], sh_ref[0]).astype(BF16)
    off = 0
    for o_ref, wd in zip(out_refs, widths):
        o_ref[...] = jnp.dot(hb, w_ref[:, off:off + wd],
                             preferred_element_type=F32).astype(o_ref.dtype)
        off += wd


def _norm_proj(x, g, sc, sh, w, widths, seq, tm, out_dtype):
    t, d = x.shape
    n = w.shape[1]
    bidx = lambda i: ((i * tm) // seq, 0, 0)
    return pl.pallas_call(
        functools.partial(_proj_kernel, widths=widths),
        grid=(t // tm,),
        in_specs=[pl.BlockSpec((tm, d), lambda i: (i, 0)),
                  pl.BlockSpec((1, d), lambda i: (0, 0)),
                  pl.BlockSpec((1, 1, d), bidx),
                  pl.BlockSpec((1, 1, d), bidx),
                  pl.BlockSpec((d, n), lambda i: (0, 0))],
        out_specs=[pl.BlockSpec((tm, wd), lambda i: (i, 0)) for wd in widths],
        out_shape=[jax.ShapeDtypeStruct((t, wd), out_dtype) for wd in widths],
        compiler_params=pltpu.CompilerParams(dimension_semantics=("parallel",),
                                             vmem_limit_bytes=VMEM_LIMIT),
        name="norm_proj",
    )(x, g, sc, sh, w)


def _rot(t, c, s):
    return t * c + pltpu.roll(t, 64, 1) * s


def _mixer_kernel(sinks_ref, rq_ref, rk_ref, rv_ref, rg_ref, aq_ref, ak_ref, av_ref,
                  ga_ref, gb_ref, cr_ref, sr_ref, ca_ref, sa_ref,
                  decay_ref, qwm_ref, kwm_ref, cdm_ref, msk_ref, gn_ref,
                  out_ref, state_ref, pk_ref, pv_ref, *, n_chunks):
    step = pl.program_id(1)

    @pl.when(step == 0)
    def _():
        state_ref[...] = jnp.zeros_like(state_ref)
        pk_ref[...] = jnp.zeros_like(pk_ref)
        pv_ref[...] = jnp.zeros_like(pv_ref)

    def chunk(c, carry):
        rows = pl.ds(pl.multiple_of(c * BLOCK, BLOCK), BLOCK)
        lane = lax.broadcasted_iota(jnp.int32, (BLOCK, LANES), 1)
        slot_is = [((lane // 32) % 2) == 0, ((lane // 32) % 2) == 1]
        low_half = lane < 64
        m_slot = [msk_ref[0:1, :], msk_ref[1:2, :]]
        row = lax.broadcasted_iota(jnp.int32, (BLOCK, 2 * BLOCK), 0)
        col = lax.broadcasted_iota(jnp.int32, (BLOCK, 2 * BLOCK), 1)
        first_key = jnp.where(jnp.logical_and(step == 0, c == 0), BLOCK, 0)
        lo = jnp.maximum(row + 1, first_key)
        valid = jnp.logical_and(col >= lo, col <= row + BLOCK)
        cr = cr_ref[rows, :]
        sr = sr_ref[rows, :]
        ca = ca_ref[rows, :]
        sa = sa_ref[rows, :]

        ya = []
        for p in range(RET_HEADS // 2):
            cs = slice(p * LANES, (p + 1) * LANES)
            q = _rot(rq_ref[rows, cs], cr, sr)
            k = _rot(rk_ref[rows, cs], cr, sr)
            qs = jnp.concatenate([q * m_slot[0], q * m_slot[1]], axis=0).astype(BF16)
            kb = (k * (RET_DK ** -0.5)).astype(BF16)
            sc = lax.dot_general(qs, kb, _NT, preferred_element_type=F32)
            st = state_ref[p]
            stb = st.astype(BF16)
            vbs = []
            for s in range(2):
                h = 2 * p + s
                hs = slice(h * RET_DV, (h + 1) * RET_DV)
                vb = rv_ref[rows, hs].astype(BF16)
                vbs.append(vb)
                sch = sc[s * BLOCK:(s + 1) * BLOCK] * decay_ref[h]
                qc = q * qwm_ref[h]
                lhs = jnp.concatenate([sch, qc], axis=1).astype(BF16)
                rhs = jnp.concatenate([vb, stb], axis=0)
                o = jnp.dot(lhs, rhs, preferred_element_type=F32)
                mu = jnp.mean(o, axis=-1, keepdims=True)
                dlt = o - mu
                var = jnp.mean(dlt * dlt, axis=-1, keepdims=True)
                y = (dlt * lax.rsqrt(var + EPS)) * gn_ref[:, hs]
                rg = rg_ref[rows, hs]
                ya.append(y * (rg * jax.nn.sigmoid(rg)))
            kw = jnp.concatenate([k * kwm_ref[2 * p], k * kwm_ref[2 * p + 1]], axis=0).astype(BF16)
            vst = jnp.concatenate(vbs, axis=0)
            upd = lax.dot_general(kw, vst, _TN, preferred_element_type=F32)
            state_ref[p] = st * cdm_ref[p] + upd

        yb = [None] * (ATT_HEADS // 2)
        kps, vps = [], []
        for pp in range(ATT_KV_HEADS // 2):
            cs = slice(pp * LANES, (pp + 1) * LANES)
            kps.append(_rot(ak_ref[rows, cs], ca, sa))
            vps.append(av_ref[rows, cs])
        for g in range(ATT_KV_HEADS):
            kp, vp = kps[g // 2], vps[g // 2]
            if g % 2 == 0:
                kd = jnp.where(slot_is[0], kp, pltpu.roll(kp, 32, 1))
                vd = jnp.where(low_half, vp, pltpu.roll(vp, 64, 1))
            else:
                kd = jnp.where(slot_is[1], kp, pltpu.roll(kp, 96, 1))
                vd = jnp.where(low_half, pltpu.roll(vp, 64, 1), vp)
            kdb = kd.astype(BF16)
            vdb = vd.astype(BF16)
            kcat = jnp.concatenate([pk_ref[g], kdb], axis=0)
            vcat = jnp.concatenate([pv_ref[g], vdb], axis=0)
            pk_ref[g] = kdb
            pv_ref[g] = vdb
            qparts = []
            for tt in range(2):
                t = 2 * g + tt
                qt = _rot(aq_ref[rows, t * LANES:(t + 1) * LANES], ca, sa)
                for s in range(2):
                    qparts.append(qt * (m_slot[s] * (ATT_HD ** -0.5)))
            qs = jnp.concatenate(qparts, axis=0).astype(BF16)
            s_all = lax.dot_general(qs, kcat, _NT, preferred_element_type=F32)
            pparts = []
            for r in range(4):
                sr_ = jnp.where(valid, s_all[r * BLOCK:(r + 1) * BLOCK], NEG_INF)
                sink = sinks_ref[4 * g + r]
                m = jnp.maximum(jnp.max(sr_, axis=-1, keepdims=True), sink)
                pe = jnp.exp(sr_ - m)
                den = jnp.sum(pe, axis=-1, keepdims=True) + jnp.exp(sink - m)
                pparts.append(pe / den)
            pb = jnp.concatenate(pparts, axis=0).astype(BF16)
            o = jnp.dot(pb, vcat, preferred_element_type=F32)
            for tt in range(2):
                yb[2 * g + tt] = jnp.where(low_half, o[(2 * tt) * BLOCK:(2 * tt + 1) * BLOCK],
                                           o[(2 * tt + 1) * BLOCK:(2 * tt + 2) * BLOCK])

        for ct in range(D_MODEL // LANES):
            cs = slice(ct * LANES, (ct + 1) * LANES)
            mrg = jax.nn.sigmoid(ga_ref[rows, cs]) * ya[ct] + jax.nn.sigmoid(gb_ref[rows, cs]) * yb[ct]
            out_ref[rows, cs] = mrg.astype(out_ref.dtype)
        return carry

    lax.fori_loop(0, n_chunks, chunk, 0)


def _mixer(proj, tables, consts, sinks, gn, batch, seq, ts):
    rq, rk, rv, rg, aq, ak, av, ga, gb = proj
    cr, sr, ca, sa = tables
    decay, qwm, kwm, cdm, msk = consts
    t = rq.shape[0]
    nsteps = seq // ts
    row_spec = lambda w: pl.BlockSpec((ts, w), lambda b, n: (b * nsteps + n, 0))
    full = lambda a: pl.BlockSpec(a.shape, lambda b, n: (0,) * a.ndim)
    return pl.pallas_call(
        functools.partial(_mixer_kernel, n_chunks=ts // BLOCK),
        grid=(batch, nsteps),
        in_specs=[pl.BlockSpec(memory_space=pltpu.SMEM)]
        + [row_spec(a.shape[1]) for a in (rq, rk, rv, rg, aq, ak, av, ga, gb, cr, sr, ca, sa)]
        + [full(a) for a in (decay, qwm, kwm, cdm, msk, gn)],
        out_specs=row_spec(D_MODEL),
        out_shape=jax.ShapeDtypeStruct((t, D_MODEL), BF16),
        scratch_shapes=[pltpu.VMEM((RET_HEADS // 2, LANES, RET_DV), F32),
                        pltpu.VMEM((ATT_KV_HEADS, BLOCK, LANES), BF16),
                        pltpu.VMEM((ATT_KV_HEADS, BLOCK, LANES), BF16)],
        compiler_params=pltpu.CompilerParams(dimension_semantics=("parallel", "arbitrary"),
                                             vmem_limit_bytes=VMEM_LIMIT),
        name="mixer",
    )(sinks, rq, rk, rv, rg, aq, ak, av, ga, gb, cr, sr, ca, sa, decay, qwm, kwm, cdm, msk, gn)


def _outproj_kernel(m_ref, w_ref, x_ref, g_ref, o_ref):
    o_ref[...] = x_ref[...] + g_ref[0] * jnp.dot(m_ref[...], w_ref[...],
                                                 preferred_element_type=F32)


def _outproj(merged, w, x, gate, seq, tm):
    t, d = x.shape
    return pl.pallas_call(
        _outproj_kernel,
        grid=(t // tm,),
        in_specs=[pl.BlockSpec((tm, d), lambda i: (i, 0)),
                  pl.BlockSpec((d, d), lambda i: (0, 0)),
                  pl.BlockSpec((tm, d), lambda i: (i, 0)),
                  pl.BlockSpec((1, 1, d), lambda i: ((i * tm) // seq, 0, 0))],
        out_specs=pl.BlockSpec((tm, d), lambda i: (i, 0)),
        out_shape=jax.ShapeDtypeStruct((t, d), F32),
        compiler_params=pltpu.CompilerParams(dimension_semantics=("parallel",),
                                             vmem_limit_bytes=VMEM_LIMIT),
        name="outproj",
    )(merged, w, x, gate)


def _oe_sort_pairs(n):
    pairs = []
    p = 1
    while p < n:
        k = p
        while k >= 1:
            for j in range(k % p, n - k, 2 * k):
                for i in range(min(k, n - j - k)):
                    if (i + j) // (2 * p) == (i + j + k) // (2 * p):
                        pairs.append((i + j, i + j + k))
            k //= 2
        p *= 2
    return pairs


def _bitonic_merge_pairs(n):
    pairs = []
    k = n // 2
    while k >= 1:
        pairs += [(i, i + k) for i in range(n) if (i & k) == 0]
        k //= 2
    return pairs


def _compare_exchange(x, pairs):
    for i, j in pairs:
        hi = jnp.maximum(x[i], x[j])
        lo = jnp.minimum(x[i], x[j])
        x[i], x[j] = hi, lo
    return x


def _all_sublanes(x, op):
    for sh in (4, 2, 1):
        x = op(x, pltpu.roll(x, sh, 0))
    return x


def _sorted_top16(blocks):
    x = _compare_exchange(list(blocks), _oe_sort_pairs(PEER_TOPK))
    for sh in (4, 2, 1):
        y = [pltpu.roll(v, sh, 0) for v in x]
        x = [jnp.maximum(x[k], y[PEER_TOPK - 1 - k]) for k in range(PEER_TOPK)]
        x = _compare_exchange(x, _bitonic_merge_pairs(PEER_TOPK))
    return x


def _next_largest(blocks, v_last):
    m = None
    for b in blocks:
        c = jnp.where(b < v_last, b, -jnp.inf)
        m = c if m is None else jnp.maximum(m, c)
    return _all_sublanes(m, jnp.maximum)


def _by_sublane(vals, sub):
    out = vals[SUBLANES - 1]
    for r in range(SUBLANES - 1):
        out = jnp.where(sub == r, vals[r], out)
    return out


def _peer_score_kernel(x_ref, g_ref, sc_ref, sh_ref, wq_ref, k1_ref, k2_ref,
                       ht_ref, tau_ref, e1_ref, e2_ref, q_scr):
    h = _modulated_norm(x_ref[...], g_ref[...], sc_ref[0], sh_ref[0])
    ht_ref[0] = h.T.astype(BF16)
    q = jnp.dot(h.astype(BF16), wq_ref[...], preferred_element_type=F32)
    for hh in range(2 * PEER_HEADS):
        q_scr[hh] = q[:, hh * LANES:(hh + 1) * LANES].astype(BF16)

    def head(hd, carry):
        s1 = lax.dot_general(k1_ref[hd], q_scr[2 * hd], _NT, preferred_element_type=F32)
        s2 = lax.dot_general(k2_ref[hd], q_scr[2 * hd + 1], _NT, preferred_element_type=F32)
        nb = PEER_NKEYS // SUBLANES
        b1 = [s1[SUBLANES * k:SUBLANES * (k + 1)] for k in range(nb)]
        b2 = [s2[SUBLANES * k:SUBLANES * (k + 1)] for k in range(nb)]
        v1 = _sorted_top16(b1)
        v2 = _sorted_top16(b2)
        sub = lax.broadcasted_iota(jnp.int32, v1[0].shape, 0)
        v2lo = _by_sublane(v2[:SUBLANES], sub)
        v2hi = _by_sublane(v2[SUBLANES:], sub)
        v1hi = _by_sublane(v1[SUBLANES:], sub)
        cands = [v1[0] + v2lo, v1[0] + v2hi] + [v1[a] + v2lo for a in range(1, SUBLANES)] \
            + [v1hi + v2[0]]
        work = list(cands)
        for r in range(PEER_TOPK):
            m = work[0]
            for c in work[1:]:
                m = jnp.maximum(m, c)
            m = _all_sublanes(m, jnp.maximum)
            work = [jnp.where(c == m, -jnp.inf, c) for c in work]
        c16 = m
        m = work[0]
        for c in work[1:]:
            m = jnp.maximum(m, c)
        c17 = jnp.maximum(_all_sublanes(m, jnp.maximum),
                          jnp.maximum(_next_largest(b1, v1[-1]) + v2[0],
                                      v1[0] + _next_largest(b2, v2[-1])))
        thr = 0.5 * (c16 + c17)
        top = v1[0] + v2[0]
        z = None
        for c in cands:
            e = jnp.where(c >= thr, jnp.exp(c - top), 0.0)
            z = e if z is None else z + e
        z = _all_sublanes(z, jnp.add)
        scale = 0.5 / z
        qv = []
        for b in range(PEER_TOPK):
            e = jnp.maximum(jnp.exp(v2[b] - v2[0]), E2_FLOOR).astype(BF16).astype(F32)
            if b:
                below = pltpu.bitcast(pltpu.bitcast(qv[-1], jnp.int32) - BF16_STEP, F32)
                e = jnp.minimum(e, below)
            qv.append(e)
        tau_blocks, e2_blocks = [], []
        for blk1, blk2 in zip(b1, b2):
            tau = thr - blk1
            tq = jnp.full_like(tau, jnp.inf)
            for b in range(PEER_TOPK):
                tq = jnp.where(v2[b] >= tau, qv[b], tq)
            eq = jnp.zeros_like(blk2)
            for b in reversed(range(PEER_TOPK)):
                eq = jnp.where(blk2 >= v2[b], qv[b], eq)
            tau_blocks.append(tq)
            e2_blocks.append(eq)
        tau_ref[0, hd] = jnp.concatenate(tau_blocks, axis=0)
        e1_ref[0, hd] = jnp.concatenate([jnp.exp(b - v1[0]) * scale for b in b1], axis=0)
        e2_ref[0, hd] = pltpu.bitcast(jnp.concatenate(e2_blocks, axis=0).astype(BF16), jnp.uint32)
        return carry

    lax.fori_loop(0, PEER_HEADS, head, 0)


def _peer_score(x, g, sc, sh, wq, k1, k2, seq, tt):
    t, d = x.shape
    bidx = lambda i: ((i * tt) // seq, 0, 0)
    full = lambda a: pl.BlockSpec(a.shape, lambda i: (0,) * a.ndim)
    hk = pl.BlockSpec((1, PEER_HEADS, PEER_NKEYS, tt), lambda i: (i, 0, 0, 0))
    hk_shape = jax.ShapeDtypeStruct((t // tt, PEER_HEADS, PEER_NKEYS, tt), F32)
    return pl.pallas_call(
        _peer_score_kernel,
        grid=(t // tt,),
        in_specs=[pl.BlockSpec((tt, d), lambda i: (i, 0)),
                  pl.BlockSpec((1, d), lambda i: (0, 0)),
                  pl.BlockSpec((1, 1, d), bidx),
                  pl.BlockSpec((1, 1, d), bidx),
                  full(wq), full(k1), full(k2)],
        out_specs=[pl.BlockSpec((1, d, tt), lambda i: (i, 0, 0)), hk, hk,
                   pl.BlockSpec((1, PEER_HEADS, PEER_NKEYS // 2, tt), lambda i: (i, 0, 0, 0))],
        out_shape=[jax.ShapeDtypeStruct((t // tt, d, tt), BF16), hk_shape, hk_shape,
                   jax.ShapeDtypeStruct((t // tt, PEER_HEADS, PEER_NKEYS // 2, tt), jnp.uint32)],
        scratch_shapes=[pltpu.VMEM((2 * PEER_HEADS, tt, LANES), BF16)],
        compiler_params=pltpu.CompilerParams(dimension_semantics=("parallel",),
                                             vmem_limit_bytes=VMEM_LIMIT),
        name="peer_score",
    )(x, g, sc, sh, wq, k1, k2)


def _rows_bf16(row):
    return jnp.broadcast_to(row, (PEER_NKEYS, LANES)).astype(BF16)


def _peer_dense_kernel(ht_ref, tau_ref, e1_ref, e2_ref, u_ref, vt_ref,
                       x_ref, g2_ref, fg_ref, o_ref,
                       acc_ref, a0_scr, a1_scr, g0_scr, g1_scr, tau_rows, e1_rows,
                       *, nsub, nchunk, final):
    j = pl.program_id(1)

    @pl.when(j == 0)
    def _():
        acc_ref[...] = jnp.zeros_like(acc_ref)

    keys_per_sub = SUB_EXPERTS // PEER_NKEYS
    nstage = nsub * nchunk

    def stage(s, a_cur, a_nxt, g_cur, g_prv):
        sp = jnp.maximum(s - 1, 0)
        cp = sp % nchunk
        acc_ref[cp] += jnp.dot(vt_ref[sp // nchunk], g_prv[...], preferred_element_type=F32)
        sn = jnp.minimum(s + 1, nstage - 1)
        urows = pl.ds(pl.multiple_of((sn // nchunk) * SUB_EXPERTS, SUB_EXPERTS), SUB_EXPERTS)
        a_nxt[...] = jnp.dot(u_ref[urows, :], ht_ref[sn % nchunk], preferred_element_type=F32)
        k = s // nchunk
        c = s % nchunk
        key0 = (j * nsub + k) * keys_per_sub
        grp = pl.ds(pl.multiple_of((key0 // SUBLANES) * SUBLANES, SUBLANES), SUBLANES)
        shift = (SUBLANES - key0 % SUBLANES) % SUBLANES
        for hd in range(PEER_HEADS):
            tau_rows[hd] = pltpu.roll(tau_ref[c, hd, grp, :], shift, 0)
            e1_rows[hd] = pltpu.roll(e1_ref[c, hd, grp, :], shift, 0)
        for tc in range(CHUNK // LANES):
            ts_ = slice(tc * LANES, (tc + 1) * LANES)
            w = [None] * keys_per_sub
            for hd in range(PEER_HEADS):
                e2v = pltpu.bitcast(e2_ref[c, hd, :, ts_], BF16)
                for r in range(keys_per_sub):
                    trow = _rows_bf16(tau_rows[hd, r:r + 1, ts_])
                    erow = _rows_bf16(e1_rows[hd, r:r + 1, ts_])
                    term = jnp.where(e2v >= trow, e2v, jnp.zeros_like(e2v)) * erow
                    w[r] = term if w[r] is None else w[r] + term
            for r in range(keys_per_sub):
                es = slice(r * PEER_NKEYS, (r + 1) * PEER_NKEYS)
                av = a_cur[es, ts_]
                act = av * (1.0 + lax.erf(av * (2.0 ** -0.5)))
                g_cur[es, ts_] = act.astype(BF16) * w[r]

    a0_scr[...] = jnp.dot(u_ref[0:SUB_EXPERTS, :], ht_ref[0], preferred_element_type=F32)
    g1_scr[...] = jnp.zeros_like(g1_scr)

    def body(m, carry):
        for i in range(0, STAGES_PER_TRIP, 2):
            stage(STAGES_PER_TRIP * m + i, a0_scr, a1_scr, g0_scr, g1_scr)
            stage(STAGES_PER_TRIP * m + i + 1, a1_scr, a0_scr, g1_scr, g0_scr)
        return carry

    lax.fori_loop(0, nstage // STAGES_PER_TRIP, body, 0)
    acc_ref[nchunk - 1] += jnp.dot(vt_ref[nsub - 1], g1_scr[...], preferred_element_type=F32)

    @pl.when(j == pl.num_programs(1) - 1)
    def _():
        for c in range(nchunk):
            rows = slice(c * CHUNK, (c + 1) * CHUNK)
            y = x_ref[rows, :] + g2_ref[0] * acc_ref[c].T
            if final:
                y = (y * lax.rsqrt(jnp.mean(y * y, axis=-1, keepdims=True) + EPS)) * fg_ref[...]
            o_ref[rows, :] = y


def _peer_dense(ht, tau, e1, e2, u, vt, x, g2, fg, seq, tt, te, final):
    t, d = x.shape
    ne = u.shape[0]
    nsub = te // SUB_EXPERTS
    nchunk = tt // CHUNK
    hk = pl.BlockSpec((nchunk, PEER_HEADS, PEER_NKEYS, CHUNK), lambda i, j: (i, 0, 0, 0))
    return pl.pallas_call(
        functools.partial(_peer_dense_kernel, nsub=nsub, nchunk=nchunk, final=final),
        grid=(t // tt, ne // te),
        in_specs=[pl.BlockSpec((nchunk, d, CHUNK), lambda i, j: (i, 0, 0)),
                  hk, hk,
                  pl.BlockSpec((nchunk, PEER_HEADS, PEER_NKEYS // 2, CHUNK), lambda i, j: (i, 0, 0, 0)),
                  pl.BlockSpec((te, d), lambda i, j: (j, 0)),
                  pl.BlockSpec((nsub, d, SUB_EXPERTS), lambda i, j: (j, 0, 0)),
                  pl.BlockSpec((tt, d), lambda i, j: (i, 0)),
                  pl.BlockSpec((1, 1, d), lambda i, j: ((i * tt) // seq, 0, 0)),
                  pl.BlockSpec((1, d), lambda i, j: (0, 0))],
        out_specs=pl.BlockSpec((tt, d), lambda i, j: (i, 0)),
        out_shape=jax.ShapeDtypeStruct((t, d), F32),
        scratch_shapes=[pltpu.VMEM((nchunk, d, CHUNK), F32),
                        pltpu.VMEM((SUB_EXPERTS, CHUNK), F32), pltpu.VMEM((SUB_EXPERTS, CHUNK), F32),
                        pltpu.VMEM((SUB_EXPERTS, CHUNK), BF16), pltpu.VMEM((SUB_EXPERTS, CHUNK), BF16),
                        pltpu.VMEM((PEER_HEADS, SUBLANES, CHUNK), F32),
                        pltpu.VMEM((PEER_HEADS, SUBLANES, CHUNK), F32)],
        compiler_params=pltpu.CompilerParams(dimension_semantics=("parallel", "arbitrary"),
                                             vmem_limit_bytes=VMEM_LIMIT),
        name="peer_dense",
    )(ht, tau, e1, e2, u, vt, x, g2, fg)


def _sub_block_t(v):
    ne, d = v.shape
    return v.astype(BF16).reshape(ne // SUB_EXPERTS, SUB_EXPERTS, d).transpose(0, 2, 1)


def _tile(n, want):
    while n % want:
        want //= 2
    return want


def kernel(x, c, positions, ada_w, ada_b, norm1_g, norm2_g, w_in, ret_gn_g, sinks, w_out,
           peer_wq, peer_k1, peer_k2, peer_u, peer_v, final_g):
    batch, seq, d = x.shape
    depth = ada_w.shape[0]
    t = batch * seq
    tm = _tile(seq, 256)
    ts = _tile(seq, 256)
    tt_score = CHUNK
    tt_dense = _tile(seq, 512)
    te = SUBLANES * PEER_NKEYS

    mod = _adaln(c, ada_w, ada_b)
    tables = _rotary_tables(positions)
    decay, qwm, kwm, cdm = _retention_consts()
    msk = jnp.asarray(_slot_masks())
    col_perm = jnp.asarray(_in_col_perm())

    xf = x.reshape(t, d)
    for l in range(depth):
        sh1, sc1, g1, sh2, sc2, g2 = [mod[l, :, i * d:(i + 1) * d].reshape(batch, 1, d)
                                      for i in range(6)]
        w_in_l = jnp.take(w_in[l], col_perm, axis=1).astype(BF16)
        proj = _norm_proj(xf, norm1_g[l].reshape(1, d), sc1, sh1, w_in_l, IN_WIDTHS, seq, tm, F32)
        merged = _mixer(proj, tables, (decay, qwm, kwm, cdm, msk), sinks[l],
                        ret_gn_g[l].reshape(1, d), batch, seq, ts)
        xf = _outproj(merged, w_out[l].astype(BF16), xf, g1, seq, tm)
        ht, tau, e1, e2 = _peer_score(
            xf, norm2_g[l].reshape(1, d), sc2, sh2, peer_wq[l].astype(BF16),
            peer_k1[l].astype(BF16), peer_k2[l].astype(BF16), seq, tt_score)
        xf = _peer_dense(ht, tau, e1, e2, peer_u[l].astype(BF16),
                         _sub_block_t(peer_v[l]), xf, g2, final_g.reshape(1, d),
                         seq, tt_dense, te, final=(l == depth - 1))
    return xf.reshape(batch, seq, d)
```

```python
import functools

import numpy as np
import jax
import jax.numpy as jnp
from jax import lax
from jax.experimental import pallas as pl
from jax.experimental.pallas import tpu as pltpu

F32 = jnp.float32
BF16 = jnp.bfloat16

D_MODEL = 1024
RET_HEADS = 8
RET_DK = 64
RET_DV = 128
RET_ROPE_BASE = 10000.0
ATT_HD = 64
ATT_HEADS = 16
ATT_KV_HEADS = 4
ROPE_DIMS = 16
ROPE_THETA = 500000.0
BLOCK = 128
PEER_HEADS = 8
PEER_NKEYS = 128
PEER_TOPK = 16
PEER_QDIM = 256
EPS = 1e-6
NEG_INF = -1e30
E2_FLOOR = 1e-30
BF16_STEP = 1 << 16
LANES = 128
SUBLANES = 8
SUB_EXPERTS = 256
CHUNK = 256
STAGES_PER_TRIP = 4
VMEM_LIMIT = 56 * 1024 * 1024

IN_WIDTHS = (512, 512, 1024, 1024, 1024, 256, 256, 1024, 1024)

_NT = (((1,), (1,)), ((), ()))
_TN = (((0,), (0,)), ((), ()))


def _pair_lane(slot, pos):
    return (pos // 32) * 64 + slot * 32 + pos % 32


def _pair_perm(n_heads, dim_order):
    src = np.zeros(n_heads * 64, np.int32)
    for p in range(n_heads // 2):
        for slot in range(2):
            for pos in range(64):
                src[p * LANES + _pair_lane(slot, pos)] = (2 * p + slot) * 64 + dim_order[pos]
    return src


def _att_dim_order():
    return np.concatenate([np.arange(0, 8), np.arange(16, 40), np.arange(8, 16), np.arange(40, 64)])


def _in_col_perm():
    ret = _pair_perm(RET_HEADS, np.arange(64))
    att_q = _pair_perm(ATT_HEADS, _att_dim_order())
    att_k = _pair_perm(ATT_KV_HEADS, _att_dim_order())
    starts = np.concatenate([[0], np.cumsum(IN_WIDTHS)[:-1]])
    cols = np.arange(sum(IN_WIDTHS), dtype=np.int32)
    cols[starts[0]:starts[0] + 512] = starts[0] + ret
    cols[starts[1]:starts[1] + 512] = starts[1] + ret
    cols[starts[4]:starts[4] + 1024] = starts[4] + att_q
    cols[starts[5]:starts[5] + 256] = starts[5] + att_k
    return cols


def _slot_masks():
    lane = np.arange(LANES)
    slot = (lane // 32) % 2
    return np.stack([(slot == 0), (slot == 1)]).astype(np.float32)


def _retention_consts():
    h = np.arange(RET_HEADS, dtype=np.float64)
    log_g = np.log(1.0 - np.power(2.0, -5.0 - h))
    idx = np.arange(BLOCK, dtype=np.float64)
    diff = idx[:, None] - idx[None, :]
    decay = np.where(diff >= 0, np.exp(log_g[:, None, None] * np.maximum(diff, 0.0)), 0.0)
    k_w = np.exp(log_g[:, None] * (BLOCK - 1.0 - idx)[None, :])
    q_w = np.exp(log_g[:, None] * (idx + 1.0)[None, :])
    cdec = np.exp(log_g * BLOCK)
    masks = _slot_masks().astype(np.float64)
    qwm = np.stack([q_w[hh][:, None] * masks[hh % 2][None, :] for hh in range(RET_HEADS)])
    kwm = np.stack([k_w[hh][:, None] * masks[hh % 2][None, :] * (RET_DK ** -0.5)
                    for hh in range(RET_HEADS)])
    row_slot = (np.arange(LANES) // 32) % 2
    cdm = np.stack([np.broadcast_to(cdec[2 * p + row_slot][:, None], (LANES, LANES))
                    for p in range(RET_HEADS // 2)])
    f = lambda a: jnp.asarray(a.astype(np.float32))
    return f(decay), f(qwm), f(kwm), f(cdm)


def _rotary_tables(positions):
    pos = positions.reshape(-1).astype(F32)
    lane = np.arange(LANES)
    hpos = (lane // 64) * 32 + lane % 32
    sign = np.where(lane < 64, -1.0, 1.0).astype(np.float32)
    half = RET_DK // 2
    inv_r = jnp.power(RET_ROPE_BASE, -jnp.arange(half, dtype=F32) * (2.0 / RET_DK))
    ang_r = pos[:, None] * inv_r[None, :]
    fr = (hpos % 32)
    cr = jnp.cos(ang_r)[:, fr]
    sr = jnp.sin(ang_r)[:, fr] * sign[None, :]
    half_a = ROPE_DIMS // 2
    inv_a = jnp.power(ROPE_THETA, -jnp.arange(half_a, dtype=F32) * (2.0 / ROPE_DIMS))
    ang_a = pos[:, None] * inv_a[None, :]
    is_rot = (hpos % 32) < half_a
    fa = np.where(is_rot, hpos % 32, 0)
    ca = jnp.where(is_rot[None, :], jnp.cos(ang_a)[:, fa], 1.0)
    sa = jnp.where(is_rot[None, :], jnp.sin(ang_a)[:, fa] * sign[None, :], 0.0)
    return cr, sr, ca, sa


def _adaln_kernel(c_ref, w_ref, b_ref, o_ref):
    c = c_ref[...]
    cond = c * jax.nn.sigmoid(c)
    o_ref[0] = jnp.dot(cond, w_ref[0], preferred_element_type=F32,
                       precision=lax.Precision.HIGHEST) + b_ref[0]


def _adaln(c, ada_w, ada_b):
    depth, d, n = ada_w.shape
    b = c.shape[0]
    return pl.pallas_call(
        _adaln_kernel,
        grid=(depth, n // d),
        in_specs=[pl.BlockSpec((b, d), lambda l, j: (0, 0)),
                  pl.BlockSpec((1, d, d), lambda l, j: (l, 0, j)),
                  pl.BlockSpec((1, 1, d), lambda l, j: (l, 0, j))],
        out_specs=pl.BlockSpec((1, b, d), lambda l, j: (l, 0, j)),
        out_shape=jax.ShapeDtypeStruct((depth, b, n), F32),
        name="adaln",
    )(c, ada_w, ada_b.reshape(depth, 1, n))


def _modulated_norm(x, g, sc, sh):
    y = x * lax.rsqrt(jnp.mean(x * x, axis=-1, keepdims=True) + EPS)
    return (y * g) * (1.0 + sc) + sh


def _proj_kernel(x_ref, g_ref, sc_ref, sh_ref, w_ref, *out_refs, widths):
    hb = _modulated_norm(x_ref[...], g_ref[...], sc_ref[0], sh_ref[0]).astype(BF16)
    off = 0
    for o_ref, wd in zip(out_refs, widths):
        o_ref[...] = jnp.dot(hb, w_ref[:, off:off + wd],
                             preferred_element_type=F32).astype(o_ref.dtype)
        off += wd


def _norm_proj(x, g, sc, sh, w, widths, seq, tm, out_dtype):
    t, d = x.shape
    n = w.shape[1]
    bidx = lambda i: ((i * tm) // seq, 0, 0)
    return pl.pallas_call(
        functools.partial(_proj_kernel, widths=widths),
        grid=(t // tm,),
        in_specs=[pl.BlockSpec((tm, d), lambda i: (i, 0)),
                  pl.BlockSpec((1, d), lambda i: (0, 0)),
                  pl.BlockSpec((1, 1, d), bidx),
                  pl.BlockSpec((1, 1, d), bidx),
                  pl.BlockSpec((d, n), lambda i: (0, 0))],
        out_specs=[pl.BlockSpec((tm, wd), lambda i: (i, 0)) for wd in widths],
        out_shape=[jax.ShapeDtypeStruct((t, wd), out_dtype) for wd in widths],
        compiler_params=pltpu.CompilerParams(dimension_semantics=("parallel",),
                                             vmem_limit_bytes=VMEM_LIMIT),
        name="norm_proj",
    )(x, g, sc, sh, w)


def _rot(t, c, s):
    return t * c + pltpu.roll(t, 64, 1) * s


def _mixer_kernel(sinks_ref, rq_ref, rk_ref, rv_ref, rg_ref, aq_ref, ak_ref, av_ref,
                  ga_ref, gb_ref, cr_ref, sr_ref, ca_ref, sa_ref,
                  decay_ref, qwm_ref, kwm_ref, cdm_ref, msk_ref, gn_ref,
                  out_ref, state_ref, pk_ref, pv_ref, *, n_chunks):
    step = pl.program_id(1)

    @pl.when(step == 0)
    def _():
        state_ref[...] = jnp.zeros_like(state_ref)
        pk_ref[...] = jnp.zeros_like(pk_ref)
        pv_ref[...] = jnp.zeros_like(pv_ref)

    def chunk(c, carry):
        rows = pl.ds(pl.multiple_of(c * BLOCK, BLOCK), BLOCK)
        lane = lax.broadcasted_iota(jnp.int32, (BLOCK, LANES), 1)
        slot_is = [((lane // 32) % 2) == 0, ((lane // 32) % 2) == 1]
        low_half = lane < 64
        m_slot = [msk_ref[0:1, :], msk_ref[1:2, :]]
        row = lax.broadcasted_iota(jnp.int32, (BLOCK, 2 * BLOCK), 0)
        col = lax.broadcasted_iota(jnp.int32, (BLOCK, 2 * BLOCK), 1)
        first_key = jnp.where(jnp.logical_and(step == 0, c == 0), BLOCK, 0)
        lo = jnp.maximum(row + 1, first_key)
        valid = jnp.logical_and(col >= lo, col <= row + BLOCK)
        cr = cr_ref[rows, :]
        sr = sr_ref[rows, :]
        ca = ca_ref[rows, :]
        sa = sa_ref[rows, :]

        ya = []
        for p in range(RET_HEADS // 2):
            cs = slice(p * LANES, (p + 1) * LANES)
            q = _rot(rq_ref[rows, cs].astype(F32), cr, sr)
            k = _rot(rk_ref[rows, cs].astype(F32), cr, sr)
            qs = jnp.concatenate([q * m_slot[0], q * m_slot[1]], axis=0).astype(BF16)
            kb = (k * (RET_DK ** -0.5)).astype(BF16)
            sc = lax.dot_general(qs, kb, _NT, preferred_element_type=F32)
            st = state_ref[p]
            stb = st.astype(BF16)
            vbs = []
            for s in range(2):
                h = 2 * p + s
                hs = slice(h * RET_DV, (h + 1) * RET_DV)
                vb = rv_ref[rows, hs].astype(BF16)
                vbs.append(vb)
                sch = sc[s * BLOCK:(s + 1) * BLOCK] * decay_ref[h]
                qc = q * qwm_ref[h]
                lhs = jnp.concatenate([sch, qc], axis=1).astype(BF16)
                rhs = jnp.concatenate([vb, stb], axis=0)
                o = jnp.dot(lhs, rhs, preferred_element_type=F32)
                mu = jnp.mean(o, axis=-1, keepdims=True)
                dlt = o - mu
                var = jnp.mean(dlt * dlt, axis=-1, keepdims=True)
                y = (dlt * lax.rsqrt(var + EPS)) * gn_ref[:, hs]
                rg = rg_ref[rows, hs].astype(F32)
                ya.append(y * (rg * jax.nn.sigmoid(rg)))
            kw = jnp.concatenate([k * kwm_ref[2 * p], k * kwm_ref[2 * p + 1]], axis=0).astype(BF16)
            vst = jnp.concatenate(vbs, axis=0)
            upd = lax.dot_general(kw, vst, _TN, preferred_element_type=F32)
            state_ref[p] = st * cdm_ref[p] + upd

        yb = [None] * (ATT_HEADS // 2)
        kps, vps = [], []
        for pp in range(ATT_KV_HEADS // 2):
            cs = slice(pp * LANES, (pp + 1) * LANES)
            kps.append(_rot(ak_ref[rows, cs].astype(F32), ca, sa))
            vps.append(av_ref[rows, cs].astype(F32))
        for g in range(ATT_KV_HEADS):
            kp, vp = kps[g // 2], vps[g // 2]
            if g % 2 == 0:
                kd = jnp.where(slot_is[0], kp, pltpu.roll(kp, 32, 1))
                vd = jnp.where(low_half, vp, pltpu.roll(vp, 64, 1))
            else:
                kd = jnp.where(slot_is[1], kp, pltpu.roll(kp, 96, 1))
                vd = jnp.where(low_half, pltpu.roll(vp, 64, 1), vp)
            kdb = kd.astype(BF16)
            vdb = vd.astype(BF16)
            kcat = jnp.concatenate([pk_ref[g], kdb], axis=0)
            vcat = jnp.concatenate([pv_ref[g], vdb], axis=0)
            pk_ref[g] = kdb
            pv_ref[g] = vdb
            qparts = []
            for tt in range(2):
                t = 2 * g + tt
                qt = _rot(aq_ref[rows, t * LANES:(t + 1) * LANES].astype(F32), ca, sa)
                for s in range(2):
                    qparts.append(qt * (m_slot[s] * (ATT_HD ** -0.5)))
            qs = jnp.concatenate(qparts, axis=0).astype(BF16)
            s_all = lax.dot_general(qs, kcat, _NT, preferred_element_type=F32)
            pparts = []
            for r in range(4):
                sr_ = jnp.where(valid, s_all[r * BLOCK:(r + 1) * BLOCK], NEG_INF)
                sink = sinks_ref[4 * g + r]
                m = jnp.maximum(jnp.max(sr_, axis=-1, keepdims=True), sink)
                pe = jnp.exp(sr_ - m)
                den = jnp.sum(pe, axis=-1, keepdims=True) + jnp.exp(sink - m)
                pparts.append(pe / den)
            pb = jnp.concatenate(pparts, axis=0).astype(BF16)
            o = jnp.dot(pb, vcat, preferred_element_type=F32)
            for tt in range(2):
                yb[2 * g + tt] = jnp.where(low_half, o[(2 * tt) * BLOCK:(2 * tt + 1) * BLOCK],
                                           o[(2 * tt + 1) * BLOCK:(2 * tt + 2) * BLOCK])

        for ct in range(D_MODEL // LANES):
            cs = slice(ct * LANES, (ct + 1) * LANES)
            mrg = (jax.nn.sigmoid(ga_ref[rows, cs].astype(F32)) * ya[ct]
                   + jax.nn.sigmoid(gb_ref[rows, cs].astype(F32)) * yb[ct])
            out_ref[rows, cs] = mrg.astype(out_ref.dtype)
        return carry

    lax.fori_loop(0, n_chunks, chunk, 0)


def _mixer(proj, tables, consts, sinks, gn, batch, seq, ts):
    rq, rk, rv, rg, aq, ak, av, ga, gb = proj
    cr, sr, ca, sa = tables
    decay, qwm, kwm, cdm, msk = consts
    t = rq.shape[0]
    nsteps = seq // ts
    row_spec = lambda w: pl.BlockSpec((ts, w), lambda b, n: (b * nsteps + n, 0))
    full = lambda a: pl.BlockSpec(a.shape, lambda b, n: (0,) * a.ndim)
    return pl.pallas_call(
        functools.partial(_mixer_kernel, n_chunks=ts // BLOCK),
        grid=(batch, nsteps),
        in_specs=[pl.BlockSpec(memory_space=pltpu.SMEM)]
        + [row_spec(a.shape[1]) for a in (rq, rk, rv, rg, aq, ak, av, ga, gb, cr, sr, ca, sa)]
        + [full(a) for a in (decay, qwm, kwm, cdm, msk, gn)],
        out_specs=row_spec(D_MODEL),
        out_shape=jax.ShapeDtypeStruct((t, D_MODEL), BF16),
        scratch_shapes=[pltpu.VMEM((RET_HEADS // 2, LANES, RET_DV), F32),
                        pltpu.VMEM((ATT_KV_HEADS, BLOCK, LANES), BF16),
                        pltpu.VMEM((ATT_KV_HEADS, BLOCK, LANES), BF16)],
        compiler_params=pltpu.CompilerParams(dimension_semantics=("parallel", "arbitrary"),
                                             vmem_limit_bytes=VMEM_LIMIT),
        name="mixer",
    )(sinks, rq, rk, rv, rg, aq, ak, av, ga, gb, cr, sr, ca, sa, decay, qwm, kwm, cdm, msk, gn)


def _outproj_kernel(m_ref, w_ref, x_ref, g_ref, o_ref):
    o_ref[...] = x_ref[...] + g_ref[0] * jnp.dot(m_ref[...], w_ref[...],
                                                 preferred_element_type=F32)


def _outproj(merged, w, x, gate, seq, tm):
    t, d = x.shape
    return pl.pallas_call(
        _outproj_kernel,
        grid=(t // tm,),
        in_specs=[pl.BlockSpec((tm, d), lambda i: (i, 0)),
                  pl.BlockSpec((d, d), lambda i: (0, 0)),
                  pl.BlockSpec((tm, d), lambda i: (i, 0)),
                  pl.BlockSpec((1, 1, d), lambda i: ((i * tm) // seq, 0, 0))],
        out_specs=pl.BlockSpec((tm, d), lambda i: (i, 0)),
        out_shape=jax.ShapeDtypeStruct((t, d), F32),
        compiler_params=pltpu.CompilerParams(dimension_semantics=("parallel",),
                                             vmem_limit_bytes=VMEM_LIMIT),
        name="outproj",
    )(merged, w, x, gate)


def _oe_sort_pairs(n):
    pairs = []
    p = 1
    while p < n:
        k = p
        while k >= 1:
            for j in range(k % p, n - k, 2 * k):
                for i in range(min(k, n - j - k)):
                    if (i + j) // (2 * p) == (i + j + k) // (2 * p):
                        pairs.append((i + j, i + j + k))
            k //= 2
        p *= 2
    return pairs


def _bitonic_merge_pairs(n):
    pairs = []
    k = n // 2
    while k >= 1:
        pairs += [(i, i + k) for i in range(n) if (i & k) == 0]
        k //= 2
    return pairs


def _compare_exchange(x, pairs):
    for i, j in pairs:
        hi = jnp.maximum(x[i], x[j])
        lo = jnp.minimum(x[i], x[j])
        x[i], x[j] = hi, lo
    return x


def _all_sublanes(x, op):
    for sh in (4, 2, 1):
        x = op(x, pltpu.roll(x, sh, 0))
    return x


def _sorted_top16(blocks):
    x = _compare_exchange(list(blocks), _oe_sort_pairs(PEER_TOPK))
    for sh in (4, 2, 1):
        y = [pltpu.roll(v, sh, 0) for v in x]
        x = [jnp.maximum(x[k], y[PEER_TOPK - 1 - k]) for k in range(PEER_TOPK)]
        x = _compare_exchange(x, _bitonic_merge_pairs(PEER_TOPK))
    return x


def _next_largest(blocks, v_last):
    m = None
    for b in blocks:
        c = jnp.where(b < v_last, b, -jnp.inf)
        m = c if m is None else jnp.maximum(m, c)
    return _all_sublanes(m, jnp.maximum)


def _by_sublane(vals, sub):
    out = vals[SUBLANES - 1]
    for r in range(SUBLANES - 1):
        out = jnp.where(sub == r, vals[r], out)
    return out


def _peer_score_kernel(x_ref, g_ref, sc_ref, sh_ref, wq_ref, k1_ref, k2_ref,
                       ht_ref, tau_ref, e1_ref, e2_ref, q_scr):
    h = _modulated_norm(x_ref[...], g_ref[...], sc_ref[0], sh_ref[0])
    ht_ref[0] = h.T.astype(BF16)
    q = jnp.dot(h.astype(BF16), wq_ref[...], preferred_element_type=F32)
    for hh in range(2 * PEER_HEADS):
        q_scr[hh] = q[:, hh * LANES:(hh + 1) * LANES].astype(BF16)

    def head(hd, carry):
        s1 = lax.dot_general(k1_ref[hd], q_scr[2 * hd], _NT, preferred_element_type=F32)
        s2 = lax.dot_general(k2_ref[hd], q_scr[2 * hd + 1], _NT, preferred_element_type=F32)
        nb = PEER_NKEYS // SUBLANES
        b1 = [s1[SUBLANES * k:SUBLANES * (k + 1)] for k in range(nb)]
        b2 = [s2[SUBLANES * k:SUBLANES * (k + 1)] for k in range(nb)]
        v1 = _sorted_top16(b1)
        v2 = _sorted_top16(b2)
        sub = lax.broadcasted_iota(jnp.int32, v1[0].shape, 0)
        v2lo = _by_sublane(v2[:SUBLANES], sub)
        v2hi = _by_sublane(v2[SUBLANES:], sub)
        v1hi = _by_sublane(v1[SUBLANES:], sub)
        cands = [v1[0] + v2lo, v1[0] + v2hi] + [v1[a] + v2lo for a in range(1, SUBLANES)] \
            + [v1hi + v2[0]]
        work = list(cands)
        for r in range(PEER_TOPK):
            m = work[0]
            for c in work[1:]:
                m = jnp.maximum(m, c)
            m = _all_sublanes(m, jnp.maximum)
            work = [jnp.where(c == m, -jnp.inf, c) for c in work]
        c16 = m
        m = work[0]
        for c in work[1:]:
            m = jnp.maximum(m, c)
        c17 = jnp.maximum(_all_sublanes(m, jnp.maximum),
                          jnp.maximum(_next_largest(b1, v1[-1]) + v2[0],
                                      v1[0] + _next_largest(b2, v2[-1])))
        thr = 0.5 * (c16 + c17)
        top = v1[0] + v2[0]
        z = None
        for c in cands:
            e = jnp.where(c >= thr, jnp.exp(c - top), 0.0)
            z = e if z is None else z + e
        z = _all_sublanes(z, jnp.add)
        scale = 0.5 / z
        qv = []
        for b in range(PEER_TOPK):
            e = jnp.maximum(jnp.exp(v2[b] - v2[0]), E2_FLOOR).astype(BF16).astype(F32)
            if b:
                below = pltpu.bitcast(pltpu.bitcast(qv[-1], jnp.int32) - BF16_STEP, F32)
                e = jnp.minimum(e, below)
            qv.append(e)
        tau_blocks, e2_blocks = [], []
        for blk1, blk2 in zip(b1, b2):
            tau = thr - blk1
            tq = jnp.full_like(tau, jnp.inf)
            for b in range(PEER_TOPK):
                tq = jnp.where(v2[b] >= tau, qv[b], tq)
            eq = jnp.zeros_like(blk2)
            for b in reversed(range(PEER_TOPK)):
                eq = jnp.where(blk2 >= v2[b], qv[b], eq)
            tau_blocks.append(tq)
            e2_blocks.append(eq)
        tau_ref[0, hd] = jnp.concatenate(tau_blocks, axis=0)
        e1_ref[0, hd] = jnp.concatenate([jnp.exp(b - v1[0]) * scale for b in b1], axis=0)
        e2_ref[0, hd] = pltpu.bitcast(jnp.concatenate(e2_blocks, axis=0).astype(BF16), jnp.uint32)
        return carry

    lax.fori_loop(0, PEER_HEADS, head, 0)


def _peer_score(x, g, sc, sh, wq, k1, k2, seq, tt):
    t, d = x.shape
    bidx = lambda i: ((i * tt) // seq, 0, 0)
    full = lambda a: pl.BlockSpec(a.shape, lambda i: (0,) * a.ndim)
    hk = pl.BlockSpec((1, PEER_HEADS, PEER_NKEYS, tt), lambda i: (i, 0, 0, 0))
    hk_shape = jax.ShapeDtypeStruct((t // tt, PEER_HEADS, PEER_NKEYS, tt), F32)
    return pl.pallas_call(
        _peer_score_kernel,
        grid=(t // tt,),
        in_specs=[pl.BlockSpec((tt, d), lambda i: (i, 0)),
                  pl.BlockSpec((1, d), lambda i: (0, 0)),
                  pl.BlockSpec((1, 1, d), bidx),
                  pl.BlockSpec((1, 1, d), bidx),
                  full(wq), full(k1), full(k2)],
        out_specs=[pl.BlockSpec((1, d, tt), lambda i: (i, 0, 0)), hk, hk,
                   pl.BlockSpec((1, PEER_HEADS, PEER_NKEYS // 2, tt), lambda i: (i, 0, 0, 0))],
        out_shape=[jax.ShapeDtypeStruct((t // tt, d, tt), BF16), hk_shape, hk_shape,
                   jax.ShapeDtypeStruct((t // tt, PEER_HEADS, PEER_NKEYS // 2, tt), jnp.uint32)],
        scratch_shapes=[pltpu.VMEM((2 * PEER_HEADS, tt, LANES), BF16)],
        compiler_params=pltpu.CompilerParams(dimension_semantics=("parallel",),
                                             vmem_limit_bytes=VMEM_LIMIT),
        name="peer_score",
    )(x, g, sc, sh, wq, k1, k2)


def _rows_bf16(row):
    return jnp.broadcast_to(row, (PEER_NKEYS, LANES)).astype(BF16)


def _peer_dense_kernel(ht_ref, tau_ref, e1_ref, e2_ref, u_ref, vt_ref,
                       x_ref, g2_ref, fg_ref, o_ref,
                       acc_ref, a0_scr, a1_scr, g0_scr, g1_scr, tau_rows, e1_rows,
                       *, nsub, nchunk, final):
    j = pl.program_id(1)

    @pl.when(j == 0)
    def _():
        acc_ref[...] = jnp.zeros_like(acc_ref)

    keys_per_sub = SUB_EXPERTS // PEER_NKEYS
    nstage = nsub * nchunk

    def stage(s, a_cur, a_nxt, g_cur, g_prv):
        sp = jnp.maximum(s - 1, 0)
        cp = sp % nchunk
        acc_ref[cp] += jnp.dot(vt_ref[sp // nchunk], g_prv[...], preferred_element_type=F32)
        sn = jnp.minimum(s + 1, nstage - 1)
        urows = pl.ds(pl.multiple_of((sn // nchunk) * SUB_EXPERTS, SUB_EXPERTS), SUB_EXPERTS)
        a_nxt[...] = jnp.dot(u_ref[urows, :], ht_ref[sn % nchunk], preferred_element_type=F32)
        k = s // nchunk
        c = s % nchunk
        key0 = (j * nsub + k) * keys_per_sub
        grp = pl.ds(pl.multiple_of((key0 // SUBLANES) * SUBLANES, SUBLANES), SUBLANES)
        shift = (SUBLANES - key0 % SUBLANES) % SUBLANES
        for hd in range(PEER_HEADS):
            tau_rows[hd] = pltpu.roll(tau_ref[c, hd, grp, :], shift, 0)
            e1_rows[hd] = pltpu.roll(e1_ref[c, hd, grp, :], shift, 0)
        for tc in range(CHUNK // LANES):
            ts_ = slice(tc * LANES, (tc + 1) * LANES)
            w = [None] * keys_per_sub
            for hd in range(PEER_HEADS):
                e2v = pltpu.bitcast(e2_ref[c, hd, :, ts_], BF16)
                for r in range(keys_per_sub):
                    trow = _rows_bf16(tau_rows[hd, r:r + 1, ts_])
                    erow = _rows_bf16(e1_rows[hd, r:r + 1, ts_])
                    term = jnp.where(e2v >= trow, e2v, jnp.zeros_like(e2v)) * erow
                    w[r] = term if w[r] is None else w[r] + term
            for r in range(keys_per_sub):
                es = slice(r * PEER_NKEYS, (r + 1) * PEER_NKEYS)
                av = a_cur[es, ts_]
                act = av * (1.0 + lax.erf(av * (2.0 ** -0.5)))
                g_cur[es, ts_] = act.astype(BF16) * w[r]

    a0_scr[...] = jnp.dot(u_ref[0:SUB_EXPERTS, :], ht_ref[0], preferred_element_type=F32)
    g1_scr[...] = jnp.zeros_like(g1_scr)

    def body(m, carry):
        for i in range(0, STAGES_PER_TRIP, 2):
            stage(STAGES_PER_TRIP * m + i, a0_scr, a1_scr, g0_scr, g1_scr)
            stage(STAGES_PER_TRIP * m + i + 1, a1_scr, a0_scr, g1_scr, g0_scr)
        return carry

    lax.fori_loop(0, nstage // STAGES_PER_TRIP, body, 0)
    acc_ref[nchunk - 1] += jnp.dot(vt_ref[nsub - 1], g1_scr[...], preferred_element_type=F32)

    @pl.when(j == pl.num_programs(1) - 1)
    def _():
        for c in range(nchunk):
            rows = slice(c * CHUNK, (c + 1) * CHUNK)
            y = x_ref[rows, :] + g2_ref[0] * acc_ref[c].T
            if final:
                y = (y * lax.rsqrt(jnp.mean(y * y, axis=-1, keepdims=True) + EPS)) * fg_ref[...]
            o_ref[rows, :] = y


def _peer_dense(ht, tau, e1, e2, u, vt, x, g2, fg, seq, tt, te, final):
    t, d = x.shape
    ne = u.shape[0]
    nsub = te // SUB_EXPERTS
    nchunk = tt // CHUNK
    hk = pl.BlockSpec((nchunk, PEER_HEADS, PEER_NKEYS, CHUNK), lambda i, j: (i, 0, 0, 0))
    return pl.pallas_call(
        functools.partial(_peer_dense_kernel, nsub=nsub, nchunk=nchunk, final=final),
        grid=(t // tt, ne // te),
        in_specs=[pl.BlockSpec((nchunk, d, CHUNK), lambda i, j: (i, 0, 0)),
                  hk, hk,
                  pl.BlockSpec((nchunk, PEER_HEADS, PEER_NKEYS // 2, CHUNK), lambda i, j: (i, 0, 0, 0)),
                  pl.BlockSpec((te, d), lambda i, j: (j, 0)),
                  pl.BlockSpec((nsub, d, SUB_EXPERTS), lambda i, j: (j, 0, 0)),
                  pl.BlockSpec((tt, d), lambda i, j: (i, 0)),
                  pl.BlockSpec((1, 1, d), lambda i, j: ((i * tt) // seq, 0, 0)),
                  pl.BlockSpec((1, d), lambda i, j: (0, 0))],
        out_specs=pl.BlockSpec((tt, d), lambda i, j: (i, 0)),
        out_shape=jax.ShapeDtypeStruct((t, d), F32),
        scratch_shapes=[pltpu.VMEM((nchunk, d, CHUNK), F32),
                        pltpu.VMEM((SUB_EXPERTS, CHUNK), F32), pltpu.VMEM((SUB_EXPERTS, CHUNK), F32),
                        pltpu.VMEM((SUB_EXPERTS, CHUNK), BF16), pltpu.VMEM((SUB_EXPERTS, CHUNK), BF16),
                        pltpu.VMEM((PEER_HEADS, SUBLANES, CHUNK), F32),
                        pltpu.VMEM((PEER_HEADS, SUBLANES, CHUNK), F32)],
        compiler_params=pltpu.CompilerParams(dimension_semantics=("parallel", "arbitrary"),
                                             vmem_limit_bytes=VMEM_LIMIT),
        name="peer_dense",
    )(ht, tau, e1, e2, u, vt, x, g2, fg)


def _sub_block_t(v):
    ne, d = v.shape
    return v.astype(BF16).reshape(ne // SUB_EXPERTS, SUB_EXPERTS, d).transpose(0, 2, 1)


def _tile(n, want):
    while n % want:
        want //= 2
    return want


def kernel(x, c, positions, ada_w, ada_b, norm1_g, norm2_g, w_in, ret_gn_g, sinks, w_out,
           peer_wq, peer_k1, peer_k2, peer_u, peer_v, final_g):
    batch, seq, d = x.shape
    depth = ada_w.shape[0]
    t = batch * seq
    tm = _tile(seq, 256)
    ts = _tile(seq, 512)
    tt_score = CHUNK
    tt_dense = _tile(seq, 512)
    te = 2 * SUBLANES * PEER_NKEYS

    mod = _adaln(c, ada_w, ada_b)
    tables = _rotary_tables(positions)
    decay, qwm, kwm, cdm = _retention_consts()
    msk = jnp.asarray(_slot_masks())
    col_perm = jnp.asarray(_in_col_perm())

    xf = x.reshape(t, d)
    for l in range(depth):
        sh1, sc1, g1, sh2, sc2, g2 = [mod[l, :, i * d:(i + 1) * d].reshape(batch, 1, d)
                                      for i in range(6)]
        w_in_l = jnp.take(w_in[l], col_perm, axis=1).astype(BF16)
        proj = _norm_proj(xf, norm1_g[l].reshape(1, d), sc1, sh1, w_in_l, IN_WIDTHS, seq, tm, BF16)
        merged = _mixer(proj, tables, (decay, qwm, kwm, cdm, msk), sinks[l],
                        ret_gn_g[l].reshape(1, d), batch, seq, ts)
        xf = _outproj(merged, w_out[l].astype(BF16), xf, g1, seq, tm)
        ht, tau, e1, e2 = _peer_score(
            xf, norm2_g[l].reshape(1, d), sc2, sh2, peer_wq[l].astype(BF16),
            peer_k1[l].astype(BF16), peer_k2[l].astype(BF16), seq, tt_score)
        xf = _peer_dense(ht, tau, e1, e2, peer_u[l].astype(BF16),
                         _sub_block_t(peer_v[l]), xf, g2, final_g.reshape(1, d),
                         seq, tt_dense, te, final=(l == depth - 1))
    return xf.reshape(batch, seq, d)
```
